```python
import jax, jax.numpy as jnp
from jax import lax
import numpy as np

D_MODEL = 1024
BATCH = 8
SEQ = 4096
DEPTH = 4

HEAD_DIM = D_MODEL // 8
ATTN_GROUPS = ((128, 1), (512, 4), (2048, 16))
HEADS_PER_GROUP = 4
N_ATTN_HEADS = HEADS_PER_GROUP * len(ATTN_GROUPS)
ATTN_WIDTH = N_ATTN_HEADS * HEAD_DIM
ATTN_OUT_WIDTH = HEADS_PER_GROUP * HEAD_DIM
N_BUCKETS = 32
REL_MAX_DISTANCE = 1024
FOURIER_GROUPS = 4
FOURIER_GROUP_DIM = D_MODEL // 8
FOURIER_WIDTH = FOURIER_GROUPS * FOURIER_GROUP_DIM
POOL_WINDOWS = (2, 4, 8, 16)
POOL_GROUP_DIM = D_MODEL // 8
POOL_WIDTH = len(POOL_WINDOWS) * POOL_GROUP_DIM
N_BRANCHES = 3
IN_COLS = 3 * ATTN_WIDTH + FOURIER_WIDTH + POOL_WIDTH + N_BRANCHES * D_MODEL
FFN_DIM = 2816
N_EXPERTS = 8
TOP_K = 2
EXPERT_DIM = 3584
EXPERT_BLOCK = 512
RMS_EPS = 1e-6
NEG_INF = -1e30

kernel_name = 'hybrid_dilated_fourier_pool_moe_encoder'


def _rms_norm(x, g):
    x32 = x.astype(jnp.float32)
    y = x32 * lax.rsqrt(jnp.mean(x32 * x32, axis=-1, keepdims=True) + RMS_EPS)
    return (y * g.astype(jnp.float32)).astype(x.dtype)


def _t5_bucket(rel):
    half = N_BUCKETS // 2
    ret = (rel > 0).astype(np.int64) * half
    n = np.abs(rel)
    max_exact = half // 2
    large = max_exact + (np.log(np.maximum(n, 1) / max_exact) / np.log(REL_MAX_DISTANCE / max_exact)
                         * (half - max_exact)).astype(np.int64)
    large = np.minimum(large, half - 1)
    return (ret + np.where(n < max_exact, n, large)).astype(np.int32)


def _dilated_window_attention(q, k, v, bias_table, window, dilation):
    B, S, H, Dh = q.shape
    P = window // (2 * dilation)
    L = S // dilation
    nb = -(-L // P)
    Lp = nb * P
    BB = B * dilation

    def to_sub(t):
        t = t.reshape(B, L, dilation, H, Dh).transpose(0, 2, 3, 1, 4).reshape(BB, H, L, Dh)
        return jnp.pad(t, ((0, 0), (0, 0), (0, Lp - L), (0, 0)))

    def to_bands(t):
        t = jnp.pad(t, ((0, 0), (0, 0), (P, P), (0, 0))).reshape(BB, H, nb + 2, P, Dh)
        return jnp.concatenate([t[:, :, :-2], t[:, :, 1:-1], t[:, :, 2:]], axis=3)

    qs = to_sub(q).reshape(BB, H, nb, P, Dh)
    kb = to_bands(to_sub(k))
    vb = to_bands(to_sub(v))

    r = np.arange(P)[:, None]
    c = np.arange(3 * P)[None, :]
    off = c - P - r
    key_sub = np.arange(nb)[:, None, None] * P + c[None] - P
    mask = (np.abs(off) <= P)[None] & (key_sub >= 0) & (key_sub < L)
    bias = bias_table.astype(jnp.float32)[_t5_bucket(off * dilation)].transpose(2, 0, 1)

    s = jnp.einsum('bhnqd,bhnkd->bhnqk', qs, kb).astype(jnp.float32) * (Dh ** -0.5)
    s = jnp.where(mask[None, None], s + bias[None, :, None], NEG_INF)
    m = jnp.max(s, axis=-1, keepdims=True)
    p = jnp.exp(s - m)
    den = jnp.sum(p, axis=-1)
    o = jnp.einsum('bhnqk,bhnkd->bhnqd', p.astype(v.dtype), vb) / den[..., None].astype(v.dtype)
    lse = m[..., 0] + jnp.log(den)

    o = o.reshape(BB, H, Lp, Dh)[:, :, :L].reshape(B, dilation, H, L, Dh)
    o = o.transpose(0, 3, 1, 2, 4).reshape(B, S, H, Dh)
    lse = lse.reshape(BB, H, Lp)[:, :, :L].reshape(B, dilation, H, L).transpose(0, 3, 1, 2).reshape(B, S, H)
    return o, lse


def _fourier_mix(u):
    B, S, _ = u.shape
    u32 = u.astype(jnp.float32).reshape(B, S, FOURIER_GROUPS, FOURIER_GROUP_DIM)
    f = jnp.fft.fft2(u32, axes=(1, 3), norm='ortho').real
    return f.reshape(B, S, FOURIER_WIDTH).astype(u.dtype)


def _multiscale_pool(u, pool_w, pool_scale):
    B, S, _ = u.shape
    u32 = u.astype(jnp.float32)
    csum = jnp.concatenate([jnp.zeros((B, 1, POOL_WIDTH), jnp.float32), jnp.cumsum(u32, axis=1)], axis=1)
    pos = np.arange(S)
    outs = []
    for g, w in enumerate(POOL_WINDOWS):
        cols = slice(g * POOL_GROUP_DIM, (g + 1) * POOL_GROUP_DIM)
        lo = np.maximum(pos - w // 2, 0)
        hi = np.minimum(pos + w // 2 - 1, S - 1)
        cnt = (hi - lo + 1).astype(np.float32)
        cg = csum[:, :, cols]
        mean = (cg[:, hi + 1] - cg[:, lo]) / cnt[None, :, None]
        outs.append(mean - u32[:, :, cols])
    d = jnp.stack(outs, axis=2).astype(u.dtype)
    y = jnp.einsum('bsgc,gcd->bsgd', d, pool_w).reshape(B, S, POOL_WIDTH)
    return y * pool_scale


def _hybrid_mixer(h, w_in, q_g, k_g, rel_bias, pool_w, pool_scale, w_ba, w_bf, w_bp, w_out):
    B, S, _ = h.shape
    z = h @ w_in
    splits = np.cumsum([ATTN_WIDTH, ATTN_WIDTH, ATTN_WIDTH, FOURIER_WIDTH, POOL_WIDTH]).tolist()
    q, k, v, uf, up, gl = jnp.split(z, splits, axis=-1)
    q = _rms_norm(q.reshape(B, S, N_ATTN_HEADS, HEAD_DIM), q_g)
    k = _rms_norm(k.reshape(B, S, N_ATTN_HEADS, HEAD_DIM), k_g)
    v = v.reshape(B, S, N_ATTN_HEADS, HEAD_DIM)

    outs, lses = [], []
    for g, (window, dilation) in enumerate(ATTN_GROUPS):
        hs = slice(g * HEADS_PER_GROUP, (g + 1) * HEADS_PER_GROUP)
        o, l = _dilated_window_attention(q[:, :, hs], k[:, :, hs], v[:, :, hs], rel_bias[:, hs], window, dilation)
        outs.append(o)
        lses.append(l)
    wts = jax.nn.softmax(jnp.stack(lses, axis=0), axis=0).astype(h.dtype)
    attn = jnp.sum(wts[..., None] * jnp.stack(outs, axis=0), axis=0).reshape(B, S, ATTN_OUT_WIDTH)

    four = _fourier_mix(uf)
    pool = _multiscale_pool(up, pool_w, pool_scale)

    gates = jax.nn.sigmoid(gl.astype(jnp.float32)).astype(h.dtype).reshape(B, S, N_BRANCHES, D_MODEL)
    merged = (gates[:, :, 0] * (attn @ w_ba) + gates[:, :, 1] * (four @ w_bf) + gates[:, :, 2] * (pool @ w_bp))
    return merged @ w_out


def _swiglu(h, w1, w3, w2):
    return (jax.nn.silu(h @ w1) * (h @ w3)) @ w2


def _moe_swiglu(h, router_w, router_b, w1, w3, w2):
    B, S, D = h.shape
    N = B * S
    NK = N * TOP_K
    t = h.reshape(N, D)
    logits = t.astype(jnp.float32) @ router_w.astype(jnp.float32) + router_b.astype(jnp.float32)
    top_val, top_idx = lax.top_k(logits, TOP_K)
    gates = jax.nn.softmax(top_val, axis=-1)
    e = top_idx.reshape(-1).astype(jnp.int32)
    order = jnp.argsort(e)
    sorted_e = e[order]
    counts = jnp.bincount(e, length=N_EXPERTS).astype(jnp.int32)
    padded = ((counts + EXPERT_BLOCK - 1) // EXPERT_BLOCK) * EXPERT_BLOCK
    start = jnp.cumsum(counts) - counts
    pend = jnp.cumsum(padded)
    pstart = pend - padded
    rank = jnp.arange(NK, dtype=jnp.int32) - start[sorted_e]
    dest_sorted = pstart[sorted_e] + rank
    n_blocks = -(-NK // EXPERT_BLOCK) + N_EXPERTS
    R = n_blocks * EXPERT_BLOCK
    row_token = jnp.full((R,), N, jnp.int32).at[dest_sorted].set((order // TOP_K).astype(jnp.int32))
    t_pad = jnp.concatenate([t, jnp.zeros((1, D), t.dtype)], axis=0)
    xs = t_pad[row_token].reshape(n_blocks, EXPERT_BLOCK, D)
    blk_start = jnp.arange(n_blocks, dtype=jnp.int32) * EXPERT_BLOCK
    blk_e = jnp.minimum(jnp.searchsorted(pend, blk_start, side='right'), N_EXPERTS - 1)

    def expert_block(args):
        xb, eb = args
        return (jax.nn.silu(xb @ w1[eb]) * (xb @ w3[eb])) @ w2[eb]

    ys = lax.map(expert_block, (xs, blk_e)).reshape(R, D)
    dest = jnp.zeros((NK,), jnp.int32).at[order].set(dest_sorted)
    y = jnp.sum(ys[dest].reshape(N, TOP_K, D) * gates[..., None].astype(ys.dtype), axis=1)
    return y.reshape(B, S, D)


def setup_inputs(seed: int = 0) -> dict:
    key = jax.random.key(seed)
    ks = jax.random.split(key, 22)
    n_dense = (DEPTH + 1) // 2
    n_moe = DEPTH // 2

    def nrm(k, shape, scale):
        return jax.random.normal(k, shape, jnp.float32) * scale

    def gain(k, shape, s=0.02):
        return 1.0 + s * jax.random.normal(k, shape, jnp.float32)

    return {
        'x': nrm(ks[0], (BATCH, SEQ, D_MODEL), 1.0),
        'norm_mix_g': gain(ks[1], (DEPTH, D_MODEL)),
        'w_in': nrm(ks[2], (DEPTH, D_MODEL, IN_COLS), D_MODEL ** -0.5),
        'q_norm_g': gain(ks[3], (DEPTH, HEAD_DIM)),
        'k_norm_g': gain(ks[4], (DEPTH, HEAD_DIM)),
        'rel_bias': nrm(ks[5], (N_BUCKETS, N_ATTN_HEADS), 0.5),
        'pool_w': nrm(ks[6], (DEPTH, len(POOL_WINDOWS), POOL_GROUP_DIM, POOL_GROUP_DIM), POOL_GROUP_DIM ** -0.5),
        'pool_scale': gain(ks[7], (DEPTH, POOL_WIDTH), 0.1),
        'w_branch_attn': nrm(ks[8], (DEPTH, ATTN_OUT_WIDTH, D_MODEL), ATTN_OUT_WIDTH ** -0.5),
        'w_branch_fourier': nrm(ks[9], (DEPTH, FOURIER_WIDTH, D_MODEL), FOURIER_WIDTH ** -0.5),
        'w_branch_pool': nrm(ks[10], (DEPTH, POOL_WIDTH, D_MODEL), POOL_WIDTH ** -0.5),
        'w_out': nrm(ks[11], (DEPTH, D_MODEL, D_MODEL), D_MODEL ** -0.5),
        'norm_ffn_g': gain(ks[12], (DEPTH, D_MODEL)),
        'ffn_w1': nrm(ks[13], (n_dense, D_MODEL, FFN_DIM), D_MODEL ** -0.5),
        'ffn_w3': nrm(ks[14], (n_dense, D_MODEL, FFN_DIM), D_MODEL ** -0.5),
        'ffn_w2': nrm(ks[15], (n_dense, FFN_DIM, D_MODEL), FFN_DIM ** -0.5),
        'router_w': nrm(ks[16], (n_moe, D_MODEL, N_EXPERTS), D_MODEL ** -0.5),
        'router_b': nrm(ks[17], (n_moe, N_EXPERTS), 0.01),
        'moe_w1': nrm(ks[18], (n_moe, N_EXPERTS, D_MODEL, EXPERT_DIM), D_MODEL ** -0.5),
        'moe_w3': nrm(ks[19], (n_moe, N_EXPERTS, D_MODEL, EXPERT_DIM), D_MODEL ** -0.5),
        'moe_w2': nrm(ks[20], (n_moe, N_EXPERTS, EXPERT_DIM, D_MODEL), EXPERT_DIM ** -0.5),
    }


def reference(x, norm_mix_g, w_in, q_norm_g, k_norm_g, rel_bias, pool_w, pool_scale, w_branch_attn,
              w_branch_fourier, w_branch_pool, w_out, norm_ffn_g, ffn_w1, ffn_w3, ffn_w2,
              router_w, router_b, moe_w1, moe_w3, moe_w2):
    for layer in range(DEPTH):
        h = _rms_norm(x, norm_mix_g[layer])
        x = x + _hybrid_mixer(h, w_in[layer], q_norm_g[layer], k_norm_g[layer], rel_bias,
                              pool_w[layer], pool_scale[layer], w_branch_attn[layer],
                              w_branch_fourier[layer], w_branch_pool[layer], w_out[layer])
        h = _rms_norm(x, norm_ffn_g[layer])
        j = layer // 2
        if layer % 2 == 0:
            x = x + _swiglu(h, ffn_w1[j], ffn_w3[j], ffn_w2[j])
        else:
            x = x + _moe_swiglu(h, router_w[j], router_b[j], moe_w1[j], moe_w3[j], moe_w2[j])
    return x
```

```python
import functools
import math

import numpy as np
import jax
import jax.numpy as jnp
from jax import lax
from jax.experimental import pallas as pl
from jax.experimental.pallas import tpu as pltpu

F32 = jnp.float32
BF16 = jnp.bfloat16

D_MODEL = 1024
HEAD_DIM = 128
ATTN_GROUPS = ((128, 1), (512, 4), (2048, 16))
HEADS_PER_GROUP = 4
N_ATTN_HEADS = HEADS_PER_GROUP * len(ATTN_GROUPS)
ATTN_WIDTH = N_ATTN_HEADS * HEAD_DIM
GROUP_WIDTH = HEADS_PER_GROUP * HEAD_DIM
QKV_WIDTH = 3 * ATTN_WIDTH
N_BUCKETS = 32
REL_MAX_DISTANCE = 1024
FOURIER_GROUPS = 4
FOURIER_WIDTH = 512
POOL_WINDOWS = (2, 4, 8, 16)
POOL_WIDTH = 512
N_BRANCHES = 3
GATE_WIDTH = N_BRANCHES * D_MODEL
IN_COLS = QKV_WIDTH + FOURIER_WIDTH + POOL_WIDTH + GATE_WIDTH
N_EXPERTS = 8
TOP_K = 2
EXPERT_BLOCK = 512
RMS_EPS = 1e-6
NEG_INF = -1e30

LANES = 128
HALF_WINDOW = 64
Q_SUB = 128
K_WIN = Q_SUB + 2 * HALF_WINDOW
POOL_PAD = 8
VMEM_LIMIT = 56 * 1024 * 1024


def _cparams(*sem):
    return pltpu.CompilerParams(dimension_semantics=sem, vmem_limit_bytes=VMEM_LIMIT)


def _resident(shape, index_map):
    return pl.BlockSpec(shape, index_map, pipeline_mode=pl.Buffered(1))


def _rms(x, gain):
    return x * lax.rsqrt(jnp.mean(x * x, axis=-1, keepdims=True) + RMS_EPS) * gain


IN_CHUNK = 512


def _in_proj_kernel(x_ref, g_ref, w_ref, qg_ref, kg_ref, qkv_ref, uf_ref, up_ref, gate_ref):
    h = _rms(x_ref[...], g_ref[...]).astype(BF16)
    for c in range(IN_COLS // IN_CHUNK):
        col = c * IN_CHUNK
        acc = jnp.dot(h, w_ref[:, col:col + IN_CHUNK], preferred_element_type=F32)
        if col < 2 * ATTN_WIDTH:
            gain = qg_ref[...] if col < ATTN_WIDTH else kg_ref[...]
            for hh in range(IN_CHUNK // HEAD_DIM):
                a = acc[:, hh * HEAD_DIM:(hh + 1) * HEAD_DIM]
                lo = col + hh * HEAD_DIM
                qkv_ref[:, lo:lo + HEAD_DIM] = _rms(a, gain).astype(BF16)
        elif col < QKV_WIDTH:
            qkv_ref[:, col:col + IN_CHUNK] = acc.astype(BF16)
        elif col < QKV_WIDTH + FOURIER_WIDTH:
            uf_ref[...] = acc.astype(BF16)
        elif col < QKV_WIDTH + FOURIER_WIDTH + POOL_WIDTH:
            up_ref[...] = acc.astype(BF16)
        else:
            lo = col - (IN_COLS - GATE_WIDTH)
            gate_ref[:, lo:lo + IN_CHUNK] = (1.0 / (1.0 + jnp.exp(-acc))).astype(BF16)


def _in_proj(x2, g, w, qg, kg, tm=512):
    n = x2.shape[0]
    row = lambda i: (i, 0)
    fix = lambda i: (0, 0)
    return pl.pallas_call(
        _in_proj_kernel,
        grid=(n // tm,),
        in_specs=[pl.BlockSpec((tm, D_MODEL), row),
                  pl.BlockSpec((1, D_MODEL), fix),
                  _resident((D_MODEL, IN_COLS), fix),
                  pl.BlockSpec((1, HEAD_DIM), fix),
                  pl.BlockSpec((1, HEAD_DIM), fix)],
        out_specs=[pl.BlockSpec((tm, QKV_WIDTH), row),
                   pl.BlockSpec((tm, FOURIER_WIDTH), row),
                   pl.BlockSpec((tm, POOL_WIDTH), row),
                   pl.BlockSpec((tm, GATE_WIDTH), row)],
        out_shape=[jax.ShapeDtypeStruct((n, QKV_WIDTH), BF16),
                   jax.ShapeDtypeStruct((n, FOURIER_WIDTH), BF16),
                   jax.ShapeDtypeStruct((n, POOL_WIDTH), BF16),
                   jax.ShapeDtypeStruct((n, GATE_WIDTH), BF16)],
        compiler_params=_cparams("parallel"),
        name="in_proj",
    )(x2, g, w, qg, kg)


def _t5_bucket(rel):
    half = N_BUCKETS // 2
    ret = (rel > 0).astype(np.int64) * half
    n = np.abs(rel)
    max_exact = half // 2
    large = max_exact + (np.log(np.maximum(n, 1) / max_exact) / np.log(REL_MAX_DISTANCE / max_exact)
                         * (half - max_exact)).astype(np.int64)
    large = np.minimum(large, half - 1)
    return (ret + np.where(n < max_exact, n, large)).astype(np.int32)


def _attn_bias(rel_bias, group):
    dilation = ATTN_GROUPS[group][1]
    r = np.arange(Q_SUB)[:, None]
    c = np.arange(K_WIN)[None, :]
    tabs = []
    heads = rel_bias[:, group * HEADS_PER_GROUP:(group + 1) * HEADS_PER_GROUP].astype(F32)
    for variant in range(3):
        off = c - variant * HALF_WINDOW - r
        valid = np.abs(off) <= HALF_WINDOW
        bucket = _t5_bucket(np.clip(off, -HALF_WINDOW, HALF_WINDOW) * dilation)
        b = jnp.transpose(heads[bucket], (2, 0, 1))
        tabs.append(jnp.where(valid[None], b, NEG_INF))
    return jnp.stack(tabs, axis=0)


def _attn_kernel(q_ref, k_ref, v_ref, bias_ref, o_ref, lse_ref, vaug_ref, *, seq, tq):
    t = pl.program_id(2)

    @pl.when(t == 0)
    def _():
        for h in range(HEADS_PER_GROUP):
            vaug_ref[:, h * 2 * HEAD_DIM:h * 2 * HEAD_DIM + HEAD_DIM] = v_ref[0, :, h * HEAD_DIM:(h + 1) * HEAD_DIM]
            vaug_ref[:, h * 2 * HEAD_DIM + HEAD_DIM:(h + 1) * 2 * HEAD_DIM] = jnp.ones((seq, HEAD_DIM), BF16)

    lane = lax.broadcasted_iota(jnp.int32, (Q_SUB, LANES), 1)
    for s in range(tq // Q_SUB):
        q0 = t * tq + s * Q_SUB
        start = pl.multiple_of(jnp.clip(q0 - HALF_WINDOW, 0, seq - K_WIN), HALF_WINDOW)
        variant = (q0 - start) // HALF_WINDOW
        lse_tile = jnp.zeros((Q_SUB, LANES), F32)
        for h in range(HEADS_PER_GROUP):
            q = q_ref[0, s * Q_SUB:(s + 1) * Q_SUB, h * HEAD_DIM:(h + 1) * HEAD_DIM]
            k = k_ref[0, pl.ds(start, K_WIN), h * HEAD_DIM:(h + 1) * HEAD_DIM]
            sc = lax.dot_general(q, k, (((1,), (1,)), ((), ())), preferred_element_type=F32)
            sc = sc + bias_ref[variant, h]
            m = jnp.max(sc, axis=-1, keepdims=True)
            p = jnp.exp(sc - m)
            pv = jnp.dot(p.astype(BF16), vaug_ref[pl.ds(start, K_WIN), h * 2 * HEAD_DIM:(h + 1) * 2 * HEAD_DIM],
                         preferred_element_type=F32)
            den = pv[:, HEAD_DIM:]
            o_ref[0, s * Q_SUB:(s + 1) * Q_SUB, h * HEAD_DIM:(h + 1) * HEAD_DIM] = (pv[:, :HEAD_DIM] / den).astype(BF16)
            lse_tile = jnp.where(lane == h, m + jnp.log(den), lse_tile)
        lse_ref[0, s * Q_SUB:(s + 1) * Q_SUB, :] = lse_tile


def _attention_group(qkv, bias, group, batch, seq_full):
    dilation = ATTN_GROUPS[group][1]
    seq = seq_full // dilation
    tq = min(512, seq)
    blocks_per_res = QKV_WIDTH // GROUP_WIDTH
    qkv_v = qkv.reshape(batch, seq, dilation * QKV_WIDTH)
    o, lse = pl.pallas_call(
        functools.partial(_attn_kernel, seq=seq, tq=tq),
        grid=(batch, dilation, seq // tq),
        in_specs=[pl.BlockSpec((1, tq, GROUP_WIDTH), lambda b, r, t: (b, t, r * blocks_per_res + group)),
                  pl.BlockSpec((1, seq, GROUP_WIDTH), lambda b, r, t: (b, 0, r * blocks_per_res + 3 + group)),
                  pl.BlockSpec((1, seq, GROUP_WIDTH), lambda b, r, t: (b, 0, r * blocks_per_res + 6 + group)),
                  pl.BlockSpec((3, HEADS_PER_GROUP, Q_SUB, K_WIN), lambda b, r, t: (0, 0, 0, 0))],
        out_specs=[pl.BlockSpec((1, tq, GROUP_WIDTH), lambda b, r, t: (b, t, r)),
                   pl.BlockSpec((1, tq, LANES), lambda b, r, t: (b, t, r))],
        out_shape=[jax.ShapeDtypeStruct((batch, seq, dilation * GROUP_WIDTH), BF16),
                   jax.ShapeDtypeStruct((batch, seq, dilation * LANES), F32)],
        scratch_shapes=[pltpu.VMEM((seq, 2 * GROUP_WIDTH), BF16)],
        compiler_params=_cparams("parallel", "parallel", "arbitrary"),
        name=f"attn_g{group}",
    )(qkv_v, qkv_v, qkv_v, bias)
    return o.reshape(batch * seq_full, GROUP_WIDTH), lse.reshape(batch * seq_full, LANES)


def _dft_tables(seq):
    c = jnp.arange(HEAD_DIM, dtype=jnp.int32)
    ang_c = ((c[:, None] * c[None, :]) % HEAD_DIM).astype(F32) * (2.0 * math.pi / HEAD_DIM)
    chan = jnp.concatenate([jnp.cos(ang_c), jnp.sin(ang_c)], axis=1) * (HEAD_DIM ** -0.5)
    s = jnp.arange(seq, dtype=jnp.int32)
    ang_s = ((s[:, None] * s[None, :]) % seq).astype(F32) * (2.0 * math.pi / seq)
    pos = jnp.concatenate([jnp.cos(ang_s), -jnp.sin(ang_s)], axis=1) * (seq ** -0.5)
    return chan.astype(BF16), pos.astype(BF16)


def _fourier_kernel(u_ref, chan_ref, pos_ref, o_ref, y_ref, *, seq):
    @pl.when(pl.program_id(1) == 0)
    def _():
        rows = 512
        for r0 in range(0, seq, rows):
            for g in range(FOURIER_GROUPS):
                y = jnp.dot(u_ref[0, r0:r0 + rows, g * HEAD_DIM:(g + 1) * HEAD_DIM], chan_ref[...],
                            preferred_element_type=F32)
                y_ref[r0:r0 + rows, g * HEAD_DIM:(g + 1) * HEAD_DIM] = y[:, :HEAD_DIM].astype(BF16)
                y_ref[seq + r0:seq + r0 + rows, g * HEAD_DIM:(g + 1) * HEAD_DIM] = y[:, HEAD_DIM:].astype(BF16)

    o_ref[0] = jnp.dot(pos_ref[...], y_ref[...], preferred_element_type=F32).astype(BF16)


def _fourier(uf, chan, pos, batch, seq, tm=512):
    u3 = uf.reshape(batch, seq, FOURIER_WIDTH)
    out = pl.pallas_call(
        functools.partial(_fourier_kernel, seq=seq),
        grid=(batch, seq // tm),
        in_specs=[pl.BlockSpec((1, seq, FOURIER_WIDTH), lambda b, m: (b, 0, 0)),
                  pl.BlockSpec((HEAD_DIM, 2 * HEAD_DIM), lambda b, m: (0, 0)),
                  pl.BlockSpec((tm, 2 * seq), lambda b, m: (m, 0))],
        out_specs=pl.BlockSpec((1, tm, FOURIER_WIDTH), lambda b, m: (b, m, 0)),
        out_shape=jax.ShapeDtypeStruct((batch, seq, FOURIER_WIDTH), BF16),
        scratch_shapes=[pltpu.VMEM((2 * seq, FOURIER_WIDTH), BF16)],
        compiler_params=_cparams("parallel", "arbitrary"),
        name="fourier",
    )(u3, chan, pos)
    return out.reshape(batch * seq, FOURIER_WIDTH)


def _pool_kernel(u_ref, w_ref, scale_ref, o_ref, pad_ref, *, seq):
    rows = 512
    zeros = jnp.zeros((POOL_PAD, HEAD_DIM), F32)
    pad_ref[0:POOL_PAD, :] = zeros
    pad_ref[POOL_PAD + seq:2 * POOL_PAD + seq, :] = zeros
    for g, window in enumerate(POOL_WINDOWS):
        cols = slice(g * HEAD_DIM, (g + 1) * HEAD_DIM)
        half = window // 2
        pad_ref[POOL_PAD:POOL_PAD + seq, :] = u_ref[0, :, cols].astype(F32)
        for r0 in range(0, seq, rows):
            base = POOL_PAD + r0
            acc = pad_ref[base - half:base - half + rows, :]
            for off in range(-half + 1, half):
                acc = acc + pad_ref[base + off:base + off + rows, :]
            pos = r0 + lax.broadcasted_iota(jnp.int32, (rows, 1), 0)
            cnt = jnp.minimum(pos + half - 1, seq - 1) - jnp.maximum(pos - half, 0) + 1
            d = acc / cnt.astype(F32) - pad_ref[base:base + rows, :]
            y = jnp.dot(d.astype(BF16), w_ref[g], preferred_element_type=F32) * scale_ref[:, cols]
            o_ref[0, r0:r0 + rows, cols] = y.astype(BF16)


def _pool(up, w, scale, batch, seq):
    u3 = up.reshape(batch, seq, POOL_WIDTH)
    out = pl.pallas_call(
        functools.partial(_pool_kernel, seq=seq),
        grid=(batch,),
        in_specs=[pl.BlockSpec((1, seq, POOL_WIDTH), lambda b: (b, 0, 0)),
                  pl.BlockSpec((len(POOL_WINDOWS), HEAD_DIM, HEAD_DIM), lambda b: (0, 0, 0)),
                  pl.BlockSpec((1, POOL_WIDTH), lambda b: (0, 0))],
        out_specs=pl.BlockSpec((1, seq, POOL_WIDTH), lambda b: (b, 0, 0)),
        out_shape=jax.ShapeDtypeStruct((batch, seq, POOL_WIDTH), BF16),
        scratch_shapes=[pltpu.VMEM((seq + 2 * POOL_PAD, HEAD_DIM), F32)],
        compiler_params=_cparams("parallel"),
        name="pool",
    )(u3, w, scale)
    return out.reshape(batch * seq, POOL_WIDTH)


def _merge_kernel(x_ref, o1_ref, o2_ref, o3_ref, l1_ref, l2_ref, l3_ref, four_ref, pool_ref, gate_ref,
                  wba_ref, wbf_ref, wbp_ref, wout_ref, gffn_ref, xo_ref, *h_refs):
    o_refs = (o1_ref, o2_ref, o3_ref)
    l_refs = (l1_ref, l2_ref, l3_ref)
    heads = []
    for j in range(HEADS_PER_GROUP):
        cols = slice(j * HEAD_DIM, (j + 1) * HEAD_DIM)
        lses = [l[:, j:j + 1] for l in l_refs]
        top = jnp.maximum(jnp.maximum(lses[0], lses[1]), lses[2])
        es = [jnp.exp(l - top) for l in lses]
        num = sum(e * o[:, cols].astype(F32) for e, o in zip(es, o_refs))
        heads.append((num / (es[0] + es[1] + es[2])).astype(BF16))
    attn = jnp.concatenate(heads, axis=1)
    a = jnp.dot(attn, wba_ref[...], preferred_element_type=F32)
    f = jnp.dot(four_ref[...], wbf_ref[...], preferred_element_type=F32)
    p = jnp.dot(pool_ref[...], wbp_ref[...], preferred_element_type=F32)
    merged = (gate_ref[:, 0:D_MODEL].astype(F32) * a
              + gate_ref[:, D_MODEL:2 * D_MODEL].astype(F32) * f
              + gate_ref[:, 2 * D_MODEL:3 * D_MODEL].astype(F32) * p)
    xn = x_ref[...] + jnp.dot(merged.astype(BF16), wout_ref[...], preferred_element_type=F32)
    xo_ref[...] = xn
    if h_refs:
        h_refs[0][...] = _rms(xn, gffn_ref[...]).astype(BF16)


def _merge(x2, os_, ls_, four, pool, gates, wba, wbf, wbp, wout, gffn, with_h, tm=512):
    n = x2.shape[0]
    row = lambda i: (i, 0)
    fix = lambda i: (0, 0)
    out_specs = [pl.BlockSpec((tm, D_MODEL), row)]
    out_shape = [jax.ShapeDtypeStruct((n, D_MODEL), F32)]
    if with_h:
        out_specs.append(pl.BlockSpec((tm, D_MODEL), row))
        out_shape.append(jax.ShapeDtypeStruct((n, D_MODEL), BF16))
    return pl.pallas_call(
        _merge_kernel,
        grid=(n // tm,),
        in_specs=[pl.BlockSpec((tm, D_MODEL), row)]
        + [pl.BlockSpec((tm, GROUP_WIDTH), row)] * 3
        + [pl.BlockSpec((tm, LANES), row)] * 3
        + [pl.BlockSpec((tm, FOURIER_WIDTH), row), pl.BlockSpec((tm, POOL_WIDTH), row),
           pl.BlockSpec((tm, GATE_WIDTH), row)]
        + [pl.BlockSpec((GROUP_WIDTH, D_MODEL), fix)] * 3
        + [pl.BlockSpec((D_MODEL, D_MODEL), fix), pl.BlockSpec((1, D_MODEL), fix)],
        out_specs=out_specs,
        out_shape=out_shape,
        compiler_params=_cparams("parallel"),
        name="merge",
    )(x2, *os_, *ls_, four, pool, gates, wba, wbf, wbp, wout, gffn)


FFN_CHUNK = 256


def _silu(a):
    return a * (1.0 / (1.0 + jnp.exp(-a)))


def _ffn_kernel(x_ref, h_ref, w1_ref, w3_ref, w2_ref, o_ref, g_ref):
    h = h_ref[...]
    ffn = w1_ref.shape[1]
    for c0 in range(0, ffn, FFN_CHUNK):
        a = jnp.dot(h, w1_ref[:, c0:c0 + FFN_CHUNK], preferred_element_type=F32)
        b = jnp.dot(h, w3_ref[:, c0:c0 + FFN_CHUNK], preferred_element_type=F32)
        g_ref[:, c0:c0 + FFN_CHUNK] = (_silu(a) * b).astype(BF16)
    o_ref[...] = x_ref[...] + jnp.dot(g_ref[...], w2_ref[...], preferred_element_type=F32)


def _dense_ffn(x2, h, w1, w3, w2, tm=512):
    n = x2.shape[0]
    ffn = w1.shape[1]
    row = lambda i: (i, 0)
    fix = lambda i: (0, 0)
    return pl.pallas_call(
        _ffn_kernel,
        grid=(n // tm,),
        in_specs=[pl.BlockSpec((tm, D_MODEL), row), pl.BlockSpec((tm, D_MODEL), row),
                  _resident((D_MODEL, ffn), fix), _resident((D_MODEL, ffn), fix),
                  _resident((ffn, D_MODEL), fix)],
        out_specs=pl.BlockSpec((tm, D_MODEL), row),
        out_shape=jax.ShapeDtypeStruct((n, D_MODEL), F32),
        scratch_shapes=[pltpu.VMEM((tm, ffn), BF16)],
        compiler_params=_cparams("parallel"),
        name="dense_ffn",
    )(x2, h, w1, w3, w2)


R_E0, R_E1, R_G0, R_G1, R_RANK0, R_RANK1 = 0, 1, 2, 3, 4, 5


def _router_kernel(x_ref, g_ref, rw_ref, rb_ref, meta_ref, cnt_ref, carry_ref, *, tm):
    i = pl.program_id(0)

    @pl.when(i == 0)
    def _():
        carry_ref[...] = jnp.zeros_like(carry_ref)

    h = _rms(x_ref[...], g_ref[...])
    logits = jnp.dot(h, rw_ref[...], preferred_element_type=F32, precision=lax.Precision.HIGHEST) + rb_ref[...]
    lane = lax.broadcasted_iota(jnp.int32, (tm, LANES), 1)
    logits = jnp.where(lane < N_EXPERTS, logits, -jnp.inf)
    v0 = jnp.max(logits, axis=-1, keepdims=True)
    e0 = jnp.min(jnp.where(logits == v0, lane, LANES), axis=-1, keepdims=True)
    rest = jnp.where(lane == e0, -jnp.inf, logits)
    v1 = jnp.max(rest, axis=-1, keepdims=True)
    e1 = jnp.min(jnp.where(rest == v1, lane, LANES), axis=-1, keepdims=True)
    t = jnp.exp(v1 - v0)
    gate0 = 1.0 / (1.0 + t)
    gate1 = t / (1.0 + t)

    picked = jnp.logical_or(lane == e0, lane == e1)
    tri = (lax.broadcasted_iota(jnp.int32, (tm, tm), 1) < lax.broadcasted_iota(jnp.int32, (tm, tm), 0))
    before = jnp.dot(tri.astype(BF16), picked.astype(BF16), preferred_element_type=F32) + carry_ref[...]
    rank0 = jnp.sum(jnp.where(lane == e0, before, 0.0), axis=-1, keepdims=True)
    rank1 = jnp.sum(jnp.where(lane == e1, before, 0.0), axis=-1, keepdims=True)
    carry_ref[...] = carry_ref[...] + jnp.sum(picked.astype(F32), axis=0, keepdims=True)

    meta = jnp.zeros((tm, LANES), F32)
    for ln, val in ((R_E0, e0.astype(F32)), (R_E1, e1.astype(F32)), (R_G0, gate0), (R_G1, gate1),
                    (R_RANK0, rank0), (R_RANK1, rank1)):
        meta = jnp.where(lane == ln, val, meta)
    meta_ref[...] = meta
    cnt_ref[...] = jnp.broadcast_to(carry_ref[...], cnt_ref.shape)


def _router(x2, g, rw, rb, tm=512):
    n = x2.shape[0]
    rw_p = jnp.zeros((D_MODEL, LANES), F32).at[:, :N_EXPERTS].set(rw.astype(F32))
    rb_p = jnp.zeros((1, LANES), F32).at[0, :N_EXPERTS].set(rb.astype(F32))
    return pl.pallas_call(
        functools.partial(_router_kernel, tm=tm),
        grid=(n // tm,),
        in_specs=[pl.BlockSpec((tm, D_MODEL), lambda i: (i, 0)),
                  pl.BlockSpec((1, D_MODEL), lambda i: (0, 0)),
                  pl.BlockSpec((D_MODEL, LANES), lambda i: (0, 0)),
                  pl.BlockSpec((1, LANES), lambda i: (0, 0))],
        out_specs=[pl.BlockSpec((tm, LANES), lambda i: (i, 0)),
                   pl.BlockSpec((8, LANES), lambda i: (0, 0))],
        out_shape=[jax.ShapeDtypeStruct((n, LANES), F32),
                   jax.ShapeDtypeStruct((8, LANES), F32)],
        scratch_shapes=[pltpu.VMEM((1, LANES), F32)],
        compiler_params=_cparams("arbitrary"),
        name="router",
    )(x2, g, rw_p, rb_p)


def _row_copy(src_hbm, dst_ref, sem, src_row, dst_row):
    return pltpu.make_async_copy(src_hbm.at[pl.ds(src_row, 1), :], dst_ref.at[pl.ds(dst_row, 1), :], sem)


def _gather_kernel(idx_ref, x_hbm, o_ref, sem, *, rows):
    def start(r, c):
        _row_copy(x_hbm, o_ref, sem, idx_ref[0, 0, r], r).start()
        return c

    def wait(r, c):
        _row_copy(x_hbm, o_ref, sem, 0, r).wait()
        return c

    lax.fori_loop(0, rows, start, 0)
    lax.fori_loop(0, rows, wait, 0)


def _gather_rows(x2, row_token, rows=EXPERT_BLOCK):
    total = row_token.shape[0]
    idx = row_token.reshape(total // rows, 1, rows)
    return pl.pallas_call(
        functools.partial(_gather_kernel, rows=rows),
        grid=(total // rows,),
        in_specs=[pl.BlockSpec((1, 1, rows), lambda j: (j, 0, 0), memory_space=pltpu.SMEM),
                  pl.BlockSpec(memory_space=pl.ANY)],
        out_specs=pl.BlockSpec((rows, D_MODEL), lambda j: (j, 0)),
        out_shape=jax.ShapeDtypeStruct((total, D_MODEL), x2.dtype),
        scratch_shapes=[pltpu.SemaphoreType.DMA(())],
        compiler_params=_cparams("arbitrary"),
        name="moe_gather",
    )(idx, x2)


MOE_CHUNK = 512


def _expert_kernel(blk_e_ref, n_used_ref, xs_ref, g_ref, w1_ref, w3_ref, w2_ref, ys_ref, act_ref):
    j = pl.program_id(0)

    @pl.when(j < n_used_ref[0])
    def _():
        h = _rms(xs_ref[...], g_ref[...]).astype(BF16)
        ffn = w1_ref.shape[2]
        for c0 in range(0, ffn, MOE_CHUNK):
            a = jnp.dot(h, w1_ref[0, :, c0:c0 + MOE_CHUNK], preferred_element_type=F32)
            b = jnp.dot(h, w3_ref[0, :, c0:c0 + MOE_CHUNK], preferred_element_type=F32)
            act_ref[:, c0:c0 + MOE_CHUNK] = (_silu(a) * b).astype(BF16)
        ys_ref[...] = jnp.dot(act_ref[...], w2_ref[0], preferred_element_type=F32)

    @pl.when(j >= n_used_ref[0])
    def _():
        ys_ref[...] = jnp.zeros_like(ys_ref)


def _experts(xs, g, w1, w3, w2, blk_e, n_used):
    total = xs.shape[0]
    ffn = w1.shape[2]
    n_blocks = total // EXPERT_BLOCK
    return pl.pallas_call(
        _expert_kernel,
        grid_spec=pltpu.PrefetchScalarGridSpec(
            num_scalar_prefetch=2,
            grid=(n_blocks,),
            in_specs=[pl.BlockSpec((EXPERT_BLOCK, D_MODEL), lambda j, be, nu: (j, 0)),
                      pl.BlockSpec((1, D_MODEL), lambda j, be, nu: (0, 0)),
                      _resident((1, D_MODEL, ffn), lambda j, be, nu: (be[j], 0, 0)),
                      _resident((1, D_MODEL, ffn), lambda j, be, nu: (be[j], 0, 0)),
                      _resident((1, ffn, D_MODEL), lambda j, be, nu: (be[j], 0, 0))],
            out_specs=pl.BlockSpec((EXPERT_BLOCK, D_MODEL), lambda j, be, nu: (j, 0)),
            scratch_shapes=[pltpu.VMEM((EXPERT_BLOCK, ffn), BF16)]),
        out_shape=jax.ShapeDtypeStruct((total, D_MODEL), F32),
        compiler_params=_cparams("arbitrary"),
        name="moe_experts",
    )(blk_e, n_used, xs, g, w1, w3, w2)


def _combine_kernel(idx_ref, x_ref, meta_ref, ys_hbm, o_ref, buf_ref, sem, *, tm):
    def start(r, c):
        _row_copy(ys_hbm, buf_ref.at[0], sem, idx_ref[0, 0, 2 * r], r).start()
        _row_copy(ys_hbm, buf_ref.at[1], sem, idx_ref[0, 0, 2 * r + 1], r).start()
        return c

    def wait(r, c):
        _row_copy(ys_hbm, buf_ref.at[0], sem, 0, r).wait()
        _row_copy(ys_hbm, buf_ref.at[1], sem, 0, r).wait()
        return c

    lax.fori_loop(0, tm, start, 0)
    lax.fori_loop(0, tm, wait, 0)
    g0 = meta_ref[:, R_G0:R_G0 + 1]
    g1 = meta_ref[:, R_G1:R_G1 + 1]
    o_ref[...] = x_ref[...] + (buf_ref[0] * g0 + buf_ref[1] * g1)


def _combine(x2, meta, ys, dest, tm=256):
    n = x2.shape[0]
    idx = dest.reshape(n // tm, 1, TOP_K * tm)
    return pl.pallas_call(
        functools.partial(_combine_kernel, tm=tm),
        grid=(n // tm,),
        in_specs=[pl.BlockSpec((1, 1, TOP_K * tm), lambda i: (i, 0, 0), memory_space=pltpu.SMEM),
                  pl.BlockSpec((tm, D_MODEL), lambda i: (i, 0)),
                  pl.BlockSpec((tm, LANES), lambda i: (i, 0)),
                  pl.BlockSpec(memory_space=pl.ANY)],
        out_specs=pl.BlockSpec((tm, D_MODEL), lambda i: (i, 0)),
        out_shape=jax.ShapeDtypeStruct((n, D_MODEL), F32),
        scratch_shapes=[pltpu.VMEM((TOP_K, tm, D_MODEL), F32), pltpu.SemaphoreType.DMA(())],
        compiler_params=_cparams("arbitrary"),
        name="moe_combine",
    )(idx, x2, meta, ys)


def _moe(x2, g, rw, rb, w1, w3, w2):
    n = x2.shape[0]
    meta, cnt = _router(x2, g, rw, rb)
    counts = cnt[0, :N_EXPERTS].astype(jnp.int32)
    padded = ((counts + EXPERT_BLOCK - 1) // EXPERT_BLOCK) * EXPERT_BLOCK
    pend = jnp.cumsum(padded)
    pstart = pend - padded
    e = meta[:, R_E0:R_E1 + 1].astype(jnp.int32)
    rank = meta[:, R_RANK0:R_RANK1 + 1].astype(jnp.int32)
    dest = jnp.sum(jnp.where(e[..., None] == jnp.arange(N_EXPERTS), pstart, 0), axis=-1) + rank
    n_blocks = -(-n * TOP_K // EXPERT_BLOCK) + N_EXPERTS
    total = n_blocks * EXPERT_BLOCK
    token = jnp.repeat(jnp.arange(n, dtype=jnp.int32), TOP_K)
    row_token = jnp.zeros((total,), jnp.int32).at[dest.reshape(-1)].set(token)
    blk_start = jnp.arange(n_blocks, dtype=jnp.int32) * EXPERT_BLOCK
    blk_e = jnp.minimum(jnp.sum(blk_start[:, None] >= pend[None, :], axis=1), N_EXPERTS - 1).astype(jnp.int32)
    n_used = (pend[-1] // EXPERT_BLOCK).astype(jnp.int32).reshape(1)
    xs = _gather_rows(x2, row_token)
    ys = _experts(xs, g, w1, w3, w2, blk_e, n_used)
    return _combine(x2, meta, ys, dest.reshape(-1))


def kernel(x, norm_mix_g, w_in, q_norm_g, k_norm_g, rel_bias, pool_w, pool_scale, w_branch_attn,
           w_branch_fourier, w_branch_pool, w_out, norm_ffn_g, ffn_w1, ffn_w3, ffn_w2,
           router_w, router_b, moe_w1, moe_w3, moe_w2):
    batch, seq, _ = x.shape
    depth = w_in.shape[0]
    n = batch * seq
    x2 = x.reshape(n, D_MODEL)
    biases = [_attn_bias(rel_bias, g) for g in range(len(ATTN_GROUPS))]
    chan, pos = _dft_tables(seq)
    row = lambda v: v.reshape(1, -1).astype(F32)
    for layer in range(depth):
        qkv, uf, up, gates = _in_proj(x2, row(norm_mix_g[layer]), w_in[layer].astype(BF16),
                                      row(q_norm_g[layer]) * (HEAD_DIM ** -0.5), row(k_norm_g[layer]))
        os_, ls_ = [], []
        for g in range(len(ATTN_GROUPS)):
            o, lse = _attention_group(qkv, biases[g], g, batch, seq)
            os_.append(o)
            ls_.append(lse)
        four = _fourier(uf, chan, pos, batch, seq)
        pool = _pool(up, pool_w[layer].astype(BF16), row(pool_scale[layer]), batch, seq)
        dense = layer % 2 == 0
        j = layer // 2
        outs = _merge(x2, os_, ls_, four, pool, gates, w_branch_attn[layer].astype(BF16),
                      w_branch_fourier[layer].astype(BF16), w_branch_pool[layer].astype(BF16),
                      w_out[layer].astype(BF16), row(norm_ffn_g[layer]), with_h=dense)
        if dense:
            x2 = _dense_ffn(outs[0], outs[1], ffn_w1[j].astype(BF16), ffn_w3[j].astype(BF16),
                            ffn_w2[j].astype(BF16))
        else:
            x2 = _moe(outs[0], row(norm_ffn_g[layer]), router_w[j], router_b[j],
                      moe_w1[j].astype(BF16), moe_w3[j].astype(BF16), moe_w2[j].astype(BF16))
    return x2.reshape(batch, seq, D_MODEL)
```

```python
import functools
import math

import numpy as np
import jax
import jax.numpy as jnp
from jax import lax
from jax.experimental import pallas as pl
from jax.experimental.pallas import tpu as pltpu

F32 = jnp.float32
BF16 = jnp.bfloat16

D_MODEL = 1024
HEAD_DIM = 128
ATTN_GROUPS = ((128, 1), (512, 4), (2048, 16))
HEADS_PER_GROUP = 4
N_ATTN_HEADS = HEADS_PER_GROUP * len(ATTN_GROUPS)
ATTN_WIDTH = N_ATTN_HEADS * HEAD_DIM
GROUP_WIDTH = HEADS_PER_GROUP * HEAD_DIM
QKV_WIDTH = 3 * ATTN_WIDTH
GROUP_QKV = 3 * GROUP_WIDTH
N_BUCKETS = 32
REL_MAX_DISTANCE = 1024
FOURIER_GROUPS = 4
FOURIER_WIDTH = 512
POOL_WINDOWS = (2, 4, 8, 16)
POOL_WIDTH = 512
N_BRANCHES = 3
GATE_WIDTH = N_BRANCHES * D_MODEL
IN_COLS = QKV_WIDTH + FOURIER_WIDTH + POOL_WIDTH + GATE_WIDTH
N_EXPERTS = 8
TOP_K = 2
EXPERT_BLOCK = 512
RMS_EPS = 1e-6
NEG_INF = -1e30

LANES = 128
HALF_WINDOW = 64
Q_SUB = 128
K_WIN = Q_SUB + 2 * HALF_WINDOW
POOL_PAD = 8
DFT_TILE = 512
DFT_EXTRA = 8
VMEM_LIMIT = 56 * 1024 * 1024


def _cparams(*sem):
    return pltpu.CompilerParams(dimension_semantics=sem, vmem_limit_bytes=VMEM_LIMIT)


def _resident(shape, index_map):
    return pl.BlockSpec(shape, index_map, pipeline_mode=pl.Buffered(1))


def _rms(x, gain):
    return x * lax.rsqrt(jnp.mean(x * x, axis=-1, keepdims=True) + RMS_EPS) * gain


IN_CHUNK = 512


def _in_proj_kernel(x_ref, g_ref, w_ref, qg_ref, kg_ref, qkv0_ref, qkv1_ref, qkv2_ref, uf_ref, up_ref, gate_ref,
                    slab_ref):
    tm = x_ref.shape[0]
    qkv_refs = (qkv0_ref, qkv1_ref, qkv2_ref)
    h = _rms(x_ref[...], g_ref[...]).astype(BF16)
    for c in range(IN_COLS // IN_CHUNK):
        col = c * IN_CHUNK
        acc = jnp.dot(h, w_ref[:, col:col + IN_CHUNK], preferred_element_type=F32)
        if col < QKV_WIDTH:
            kind, group = divmod(c, len(ATTN_GROUPS))
            dilation = ATTN_GROUPS[group][1]
            out_ref = qkv_refs[group]
            for hh in range(HEADS_PER_GROUP):
                a = acc[:, hh * HEAD_DIM:(hh + 1) * HEAD_DIM]
                if kind < 2:
                    a = _rms(a, qg_ref[...] if kind == 0 else kg_ref[...])
                lo = kind * GROUP_WIDTH + hh * HEAD_DIM
                if dilation == 1:
                    out_ref[:, lo:lo + HEAD_DIM] = a.astype(BF16)
                else:
                    slab_ref[hh] = a
                    for r in range(dilation):
                        piece = slab_ref[hh, pl.ds(r, tm // dilation, stride=dilation), :]
                        out_ref[:, r * GROUP_QKV + lo:r * GROUP_QKV + lo + HEAD_DIM] = piece.astype(BF16)
        elif col < QKV_WIDTH + FOURIER_WIDTH:
            uf_ref[...] = acc.astype(BF16)
        elif col < QKV_WIDTH + FOURIER_WIDTH + POOL_WIDTH:
            up_ref[...] = acc.astype(BF16)
        else:
            lo = col - (IN_COLS - GATE_WIDTH)
            gate_ref[:, lo:lo + IN_CHUNK] = (1.0 / (1.0 + jnp.exp(-acc))).astype(BF16)


def _in_proj(x2, g, w, qg, kg, tm=512):
    n = x2.shape[0]
    row = lambda i: (i, 0)
    fix = lambda i: (0, 0)
    return pl.pallas_call(
        _in_proj_kernel,
        grid=(n // tm,),
        in_specs=[pl.BlockSpec((tm, D_MODEL), row),
                  pl.BlockSpec((1, D_MODEL), fix),
                  _resident((D_MODEL, IN_COLS), fix),
                  pl.BlockSpec((1, HEAD_DIM), fix),
                  pl.BlockSpec((1, HEAD_DIM), fix)],
        out_specs=[pl.BlockSpec((tm // d, d * GROUP_QKV), row) for _, d in ATTN_GROUPS]
        + [pl.BlockSpec((tm, FOURIER_WIDTH), row),
           pl.BlockSpec((tm, POOL_WIDTH), row),
           pl.BlockSpec((tm, GATE_WIDTH), row)],
        out_shape=[jax.ShapeDtypeStruct((n // d, d * GROUP_QKV), BF16) for _, d in ATTN_GROUPS]
        + [jax.ShapeDtypeStruct((n, FOURIER_WIDTH), BF16),
           jax.ShapeDtypeStruct((n, POOL_WIDTH), BF16),
           jax.ShapeDtypeStruct((n, GATE_WIDTH), BF16)],
        scratch_shapes=[pltpu.VMEM((HEADS_PER_GROUP, tm, HEAD_DIM), F32)],
        compiler_params=_cparams("parallel"),
        name="in_proj",
    )(x2, g, w, qg, kg)


def _t5_bucket(rel):
    half = N_BUCKETS // 2
    ret = (rel > 0).astype(np.int64) * half
    n = np.abs(rel)
    max_exact = half // 2
    large = max_exact + (np.log(np.maximum(n, 1) / max_exact) / np.log(REL_MAX_DISTANCE / max_exact)
                         * (half - max_exact)).astype(np.int64)
    large = np.minimum(large, half - 1)
    return (ret + np.where(n < max_exact, n, large)).astype(np.int32)


def _attn_bias(rel_bias, group):
    dilation = ATTN_GROUPS[group][1]
    r = np.arange(Q_SUB)[:, None]
    c = np.arange(K_WIN)[None, :]
    heads = rel_bias[:, group * HEADS_PER_GROUP:(group + 1) * HEADS_PER_GROUP].astype(F32)
    off = np.stack([c - variant * HALF_WINDOW - r for variant in range(3)])
    valid = np.abs(off) <= HALF_WINDOW
    bucket = _t5_bucket(np.clip(off, -HALF_WINDOW, HALF_WINDOW) * dilation)
    onehot = (jnp.asarray(bucket)[..., None] == jnp.arange(N_BUCKETS)).astype(F32)
    b = jnp.einsum("vqkn,nh->vhqk", onehot, heads, precision=lax.Precision.HIGHEST)
    return jnp.where(valid[:, None], b, NEG_INF)


def _attn_kernel(q_ref, k_ref, v_ref, bias_ref, o_ref, lse_ref, vaug_ref, *, seq, tq):
    t = pl.program_id(2)

    @pl.when(t == 0)
    def _():
        for h in range(HEADS_PER_GROUP):
            vaug_ref[:, h * 2 * HEAD_DIM:h * 2 * HEAD_DIM + HEAD_DIM] = v_ref[0, :, h * HEAD_DIM:(h + 1) * HEAD_DIM]
            vaug_ref[:, h * 2 * HEAD_DIM + HEAD_DIM:(h + 1) * 2 * HEAD_DIM] = jnp.ones((seq, HEAD_DIM), BF16)

    lane = lax.broadcasted_iota(jnp.int32, (Q_SUB, LANES), 1)
    for s in range(tq // Q_SUB):
        q0 = t * tq + s * Q_SUB
        start = pl.multiple_of(jnp.clip(q0 - HALF_WINDOW, 0, seq - K_WIN), HALF_WINDOW)
        variant = (q0 - start) // HALF_WINDOW
        lse_tile = jnp.zeros((Q_SUB, LANES), F32)
        for h in range(HEADS_PER_GROUP):
            q = q_ref[0, s * Q_SUB:(s + 1) * Q_SUB, h * HEAD_DIM:(h + 1) * HEAD_DIM]
            k = k_ref[0, pl.ds(start, K_WIN), h * HEAD_DIM:(h + 1) * HEAD_DIM]
            sc = lax.dot_general(q, k, (((1,), (1,)), ((), ())), preferred_element_type=F32)
            sc = sc + bias_ref[variant, h]
            m = jnp.max(sc, axis=-1, keepdims=True)
            p = jnp.exp(sc - m)
            pv = jnp.dot(p.astype(BF16), vaug_ref[pl.ds(start, K_WIN), h * 2 * HEAD_DIM:(h + 1) * 2 * HEAD_DIM],
                         preferred_element_type=F32)
            den = pv[:, HEAD_DIM:]
            o_ref[0, s * Q_SUB:(s + 1) * Q_SUB, h * HEAD_DIM:(h + 1) * HEAD_DIM] = (pv[:, :HEAD_DIM] / den).astype(BF16)
            lse_tile = jnp.where(lane == h, m + jnp.log(den), lse_tile)
        lse_ref[0, s * Q_SUB:(s + 1) * Q_SUB, :] = lse_tile


def _attention_group(qkv, bias, group, batch, seq_full):
    dilation = ATTN_GROUPS[group][1]
    seq = seq_full // dilation
    tq = min(512, seq)
    qkv_v = qkv.reshape(batch, seq, dilation * GROUP_QKV)
    o, lse = pl.pallas_call(
        functools.partial(_attn_kernel, seq=seq, tq=tq),
        grid=(batch, dilation, seq // tq),
        in_specs=[pl.BlockSpec((1, tq, GROUP_WIDTH), lambda b, r, t: (b, t, 3 * r)),
                  pl.BlockSpec((1, seq, GROUP_WIDTH), lambda b, r, t: (b, 0, 3 * r + 1)),
                  pl.BlockSpec((1, seq, GROUP_WIDTH), lambda b, r, t: (b, 0, 3 * r + 2)),
                  pl.BlockSpec((3, HEADS_PER_GROUP, Q_SUB, K_WIN), lambda b, r, t: (0, 0, 0, 0))],
        out_specs=[pl.BlockSpec((1, tq, GROUP_WIDTH), lambda b, r, t: (b, t, r)),
                   pl.BlockSpec((1, tq, LANES), lambda b, r, t: (b, t, r))],
        out_shape=[jax.ShapeDtypeStruct((batch, seq, dilation * GROUP_WIDTH), BF16),
                   jax.ShapeDtypeStruct((batch, seq, dilation * LANES), F32)],
        scratch_shapes=[pltpu.VMEM((seq, 2 * GROUP_WIDTH), BF16)],
        compiler_params=_cparams("parallel", "parallel", "arbitrary"),
        name=f"attn_g{group}",
    )(qkv_v, qkv_v, qkv_v, bias)
    return o.reshape(batch * seq, dilation * GROUP_WIDTH), lse.reshape(batch * seq, dilation * LANES)


def _dft_tables(seq):
    c = jnp.arange(HEAD_DIM, dtype=jnp.int32)
    ang_c = ((c[:, None] * c[None, :]) % HEAD_DIM).astype(F32) * (2.0 * math.pi / HEAD_DIM)
    chan = jnp.concatenate([jnp.cos(ang_c), jnp.sin(ang_c)], axis=1) * (HEAD_DIM ** -0.5)
    tiles = seq // 2 // DFT_TILE
    k = (jnp.arange(tiles, dtype=jnp.int32)[:, None] * DFT_TILE
         + jnp.arange(DFT_TILE + DFT_EXTRA, dtype=jnp.int32)[None, :])
    s = jnp.arange(seq, dtype=jnp.int32)
    ang = ((k[..., None] * s) % seq).astype(F32) * (2.0 * math.pi / seq)
    cos_t = jnp.cos(ang) * (seq ** -0.5)
    sin_t = jnp.sin(ang) * (seq ** -0.5)
    j = np.arange(DFT_TILE)[:, None]
    flip = (np.arange(DFT_TILE + DFT_EXTRA)[None, :] == DFT_TILE - j).astype(np.float32)
    return chan.astype(BF16), cos_t.astype(BF16), sin_t.astype(BF16), jnp.asarray(flip, BF16)


def _fourier_kernel(u_ref, chan_ref, cos_ref, sin_ref, flip_ref, o_ref, y_ref, *, seq):
    m = pl.program_id(1)

    @pl.when(m == 0)
    def _():
        rows = 512
        for r0 in range(0, seq, rows):
            for g in range(FOURIER_GROUPS):
                y = jnp.dot(u_ref[0, r0:r0 + rows, g * HEAD_DIM:(g + 1) * HEAD_DIM], chan_ref[...],
                            preferred_element_type=F32)
                y_ref[r0:r0 + rows, g * HEAD_DIM:(g + 1) * HEAD_DIM] = y[:, :HEAD_DIM].astype(BF16)
                y_ref[seq + r0:seq + r0 + rows, g * HEAD_DIM:(g + 1) * HEAD_DIM] = y[:, HEAD_DIM:].astype(BF16)

    even = jnp.dot(cos_ref[0], y_ref[0:seq, :], preferred_element_type=F32)
    odd = jnp.dot(sin_ref[0], y_ref[seq:2 * seq, :], preferred_element_type=F32)
    lo = pl.multiple_of(m * DFT_TILE, DFT_TILE)
    o_ref[0, pl.ds(lo, DFT_TILE), :] = (even - odd)[:DFT_TILE].astype(BF16)
    mirrored = jnp.dot(flip_ref[...], (even + odd).astype(BF16), preferred_element_type=F32)
    hi = pl.multiple_of(seq - lo - DFT_TILE, DFT_TILE)
    o_ref[0, pl.ds(hi, DFT_TILE), :] = mirrored.astype(BF16)


def _fourier(uf, tables, batch, seq):
    chan, cos_t, sin_t, flip = tables
    u3 = uf.reshape(batch, seq, FOURIER_WIDTH)
    rows = DFT_TILE + DFT_EXTRA
    out = pl.pallas_call(
        functools.partial(_fourier_kernel, seq=seq),
        grid=(batch, seq // 2 // DFT_TILE),
        in_specs=[pl.BlockSpec((1, seq, FOURIER_WIDTH), lambda b, m: (b, 0, 0)),
                  pl.BlockSpec((HEAD_DIM, 2 * HEAD_DIM), lambda b, m: (0, 0)),
                  pl.BlockSpec((1, rows, seq), lambda b, m: (m, 0, 0)),
                  pl.BlockSpec((1, rows, seq), lambda b, m: (m, 0, 0)),
                  pl.BlockSpec((DFT_TILE, rows), lambda b, m: (0, 0))],
        out_specs=pl.BlockSpec((1, seq, FOURIER_WIDTH), lambda b, m: (b, 0, 0)),
        out_shape=jax.ShapeDtypeStruct((batch, seq, FOURIER_WIDTH), BF16),
        scratch_shapes=[pltpu.VMEM((2 * seq, FOURIER_WIDTH), BF16)],
        compiler_params=_cparams("parallel", "arbitrary"),
        name="fourier",
    )(u3, chan, cos_t, sin_t, flip)
    return out.reshape(batch * seq, FOURIER_WIDTH)


def _pool_kernel(u_ref, w_ref, scale_ref, o_ref, pad_ref, *, seq):
    rows = 512
    zeros = jnp.zeros((POOL_PAD, HEAD_DIM), F32)
    pad_ref[0:POOL_PAD, :] = zeros
    pad_ref[POOL_PAD + seq:2 * POOL_PAD + seq, :] = zeros
    for g, window in enumerate(POOL_WINDOWS):
        cols = slice(g * HEAD_DIM, (g + 1) * HEAD_DIM)
        half = window // 2
        pad_ref[POOL_PAD:POOL_PAD + seq, :] = u_ref[0, :, cols].astype(F32)
        for r0 in range(0, seq, rows):
            base = POOL_PAD + r0
            acc = pad_ref[base - half:base - half + rows, :]
            for off in range(-half + 1, half):
                acc = acc + pad_ref[base + off:base + off + rows, :]
            pos = r0 + lax.broadcasted_iota(jnp.int32, (rows, 1), 0)
            cnt = jnp.minimum(pos + half - 1, seq - 1) - jnp.maximum(pos - half, 0) + 1
            d = acc / cnt.astype(F32) - pad_ref[base:base + rows, :]
            y = jnp.dot(d.astype(BF16), w_ref[g], preferred_element_type=F32) * scale_ref[:, cols]
            o_ref[0, r0:r0 + rows, cols] = y.astype(BF16)


def _pool(up, w, scale, batch, seq):
    u3 = up.reshape(batch, seq, POOL_WIDTH)
    out = pl.pallas_call(
        functools.partial(_pool_kernel, seq=seq),
        grid=(batch,),
        in_specs=[pl.BlockSpec((1, seq, POOL_WIDTH), lambda b: (b, 0, 0)),
                  pl.BlockSpec((len(POOL_WINDOWS), HEAD_DIM, HEAD_DIM), lambda b: (0, 0, 0)),
                  pl.BlockSpec((1, POOL_WIDTH), lambda b: (0, 0))],
        out_specs=pl.BlockSpec((1, seq, POOL_WIDTH), lambda b: (b, 0, 0)),
        out_shape=jax.ShapeDtypeStruct((batch, seq, POOL_WIDTH), BF16),
        scratch_shapes=[pltpu.VMEM((seq + 2 * POOL_PAD, HEAD_DIM), F32)],
        compiler_params=_cparams("parallel"),
        name="pool",
    )(u3, w, scale)
    return out.reshape(batch * seq, POOL_WIDTH)


def _merge_kernel(x_ref, o1_ref, o2_ref, o3_ref, l1_ref, l2_ref, l3_ref, four_ref, pool_ref, gate_ref,
                  wba_ref, wbf_ref, wbp_ref, wout_ref, gffn_ref, xo_ref, *rest):
    h_refs, (oslab_ref, lslab_ref) = rest[:-2], rest[-2:]
    tm = x_ref.shape[0]
    o_refs = (o1_ref, o2_ref, o3_ref)
    l_refs = (l1_ref, l2_ref, l3_ref)
    for g, (_, d) in enumerate(ATTN_GROUPS):
        if d == 1:
            continue
        for r in range(d):
            rows = pl.ds(r, tm // d, stride=d)
            lslab_ref[g - 1, rows, :] = l_refs[g][:, r * LANES:(r + 1) * LANES]
            for j in range(HEADS_PER_GROUP):
                lo = r * GROUP_WIDTH + j * HEAD_DIM
                oslab_ref[g - 1, j, rows, :] = o_refs[g][:, lo:lo + HEAD_DIM].astype(F32)
    heads = []
    for j in range(HEADS_PER_GROUP):
        cols = slice(j * HEAD_DIM, (j + 1) * HEAD_DIM)
        lses = [l1_ref[:, j:j + 1], lslab_ref[0, :, j:j + 1], lslab_ref[1, :, j:j + 1]]
        outs = [o1_ref[:, cols].astype(F32), oslab_ref[0, j], oslab_ref[1, j]]
        top = jnp.maximum(jnp.maximum(lses[0], lses[1]), lses[2])
        es = [jnp.exp(l - top) for l in lses]
        num = sum(e * o for e, o in zip(es, outs))
        heads.append((num / (es[0] + es[1] + es[2])).astype(BF16))
    attn = jnp.concatenate(heads, axis=1)
    a = jnp.dot(attn, wba_ref[...], preferred_element_type=F32)
    f = jnp.dot(four_ref[...], wbf_ref[...], preferred_element_type=F32)
    p = jnp.dot(pool_ref[...], wbp_ref[...], preferred_element_type=F32)
    merged = (gate_ref[:, 0:D_MODEL].astype(F32) * a
              + gate_ref[:, D_MODEL:2 * D_MODEL].astype(F32) * f
              + gate_ref[:, 2 * D_MODEL:3 * D_MODEL].astype(F32) * p)
    xn = x_ref[...] + jnp.dot(merged.astype(BF16), wout_ref[...], preferred_element_type=F32)
    xo_ref[...] = xn
    if h_refs:
        h_refs[0][...] = _rms(xn, gffn_ref[...]).astype(BF16)


def _merge(x2, os_, ls_, four, pool, gates, wba, wbf, wbp, wout, gffn, with_h, tm=512):
    n = x2.shape[0]
    row = lambda i: (i, 0)
    fix = lambda i: (0, 0)
    out_specs = [pl.BlockSpec((tm, D_MODEL), row)]
    out_shape = [jax.ShapeDtypeStruct((n, D_MODEL), F32)]
    if with_h:
        out_specs.append(pl.BlockSpec((tm, D_MODEL), row))
        out_shape.append(jax.ShapeDtypeStruct((n, D_MODEL), BF16))
    return pl.pallas_call(
        _merge_kernel,
        grid=(n // tm,),
        in_specs=[pl.BlockSpec((tm, D_MODEL), row)]
        + [pl.BlockSpec((tm // d, d * GROUP_WIDTH), row) for _, d in ATTN_GROUPS]
        + [pl.BlockSpec((tm // d, d * LANES), row) for _, d in ATTN_GROUPS]
        + [pl.BlockSpec((tm, FOURIER_WIDTH), row), pl.BlockSpec((tm, POOL_WIDTH), row),
           pl.BlockSpec((tm, GATE_WIDTH), row)]
        + [pl.BlockSpec((GROUP_WIDTH, D_MODEL), fix)] * 3
        + [pl.BlockSpec((D_MODEL, D_MODEL), fix), pl.BlockSpec((1, D_MODEL), fix)],
        out_specs=out_specs,
        out_shape=out_shape,
        scratch_shapes=[pltpu.VMEM((len(ATTN_GROUPS) - 1, HEADS_PER_GROUP, tm, HEAD_DIM), F32),
                        pltpu.VMEM((len(ATTN_GROUPS) - 1, tm, LANES), F32)],
        compiler_params=_cparams("parallel"),
        name="merge",
    )(x2, *os_, *ls_, four, pool, gates, wba, wbf, wbp, wout, gffn)


FFN_CHUNK = 256


def _silu(a):
    return a * (1.0 / (1.0 + jnp.exp(-a)))


def _ffn_kernel(x_ref, h_ref, w1_ref, w3_ref, w2_ref, o_ref, g_ref):
    h = h_ref[...]
    ffn = w1_ref.shape[1]
    for c0 in range(0, ffn, FFN_CHUNK):
        a = jnp.dot(h, w1_ref[:, c0:c0 + FFN_CHUNK], preferred_element_type=F32)
        b = jnp.dot(h, w3_ref[:, c0:c0 + FFN_CHUNK], preferred_element_type=F32)
        g_ref[:, c0:c0 + FFN_CHUNK] = (_silu(a) * b).astype(BF16)
    o_ref[...] = x_ref[...] + jnp.dot(g_ref[...], w2_ref[...], preferred_element_type=F32)


def _dense_ffn(x2, h, w1, w3, w2, tm=512):
    n = x2.shape[0]
    ffn = w1.shape[1]
    row = lambda i: (i, 0)
    fix = lambda i: (0, 0)
    return pl.pallas_call(
        _ffn_kernel,
        grid=(n // tm,),
        in_specs=[pl.BlockSpec((tm, D_MODEL), row), pl.BlockSpec((tm, D_MODEL), row),
                  _resident((D_MODEL, ffn), fix), _resident((D_MODEL, ffn), fix),
                  _resident((ffn, D_MODEL), fix)],
        out_specs=pl.BlockSpec((tm, D_MODEL), row),
        out_shape=jax.ShapeDtypeStruct((n, D_MODEL), F32),
        scratch_shapes=[pltpu.VMEM((tm, ffn), BF16)],
        compiler_params=_cparams("parallel"),
        name="dense_ffn",
    )(x2, h, w1, w3, w2)


R_E0, R_E1, R_G0, R_G1, R_RANK0, R_RANK1 = 0, 1, 2, 3, 4, 5


def _router_kernel(x_ref, g_ref, rw_ref, rb_ref, meta_ref, metat_ref, cnt_ref, *, tm):
    h = _rms(x_ref[...], g_ref[...])
    logits = jnp.dot(h, rw_ref[...], preferred_element_type=F32, precision=lax.Precision.HIGHEST) + rb_ref[...]
    lane = lax.broadcasted_iota(jnp.int32, (tm, LANES), 1)
    logits = jnp.where(lane < N_EXPERTS, logits, -jnp.inf)
    v0 = jnp.max(logits, axis=-1, keepdims=True)
    e0 = jnp.min(jnp.where(logits == v0, lane, LANES), axis=-1, keepdims=True)
    rest = jnp.where(lane == e0, -jnp.inf, logits)
    v1 = jnp.max(rest, axis=-1, keepdims=True)
    e1 = jnp.min(jnp.where(rest == v1, lane, LANES), axis=-1, keepdims=True)
    t = jnp.exp(v1 - v0)
    gate0 = 1.0 / (1.0 + t)
    gate1 = t / (1.0 + t)

    picked = jnp.logical_or(lane == e0, lane == e1)
    tri = (lax.broadcasted_iota(jnp.int32, (tm, tm), 1) < lax.broadcasted_iota(jnp.int32, (tm, tm), 0))
    before = jnp.dot(tri.astype(BF16), picked.astype(BF16), preferred_element_type=F32)
    rank0 = jnp.sum(jnp.where(lane == e0, before, 0.0), axis=-1, keepdims=True)
    rank1 = jnp.sum(jnp.where(lane == e1, before, 0.0), axis=-1, keepdims=True)

    meta = jnp.zeros((tm, LANES), F32)
    for ln, val in ((R_E0, e0.astype(F32)), (R_E1, e1.astype(F32)), (R_G0, gate0), (R_G1, gate1),
                    (R_RANK0, rank0), (R_RANK1, rank1)):
        meta = jnp.where(lane == ln, val, meta)
    meta_ref[...] = meta
    metat_ref[0] = jnp.transpose(meta)[:8]
    cnt_ref[0] = jnp.sum(picked.astype(F32), axis=0, keepdims=True)


def _router(x2, g, rw, rb, tm):
    n = x2.shape[0]
    rw_p = jnp.zeros((D_MODEL, LANES), F32).at[:, :N_EXPERTS].set(rw.astype(F32))
    rb_p = jnp.zeros((1, LANES), F32).at[0, :N_EXPERTS].set(rb.astype(F32))
    return pl.pallas_call(
        functools.partial(_router_kernel, tm=tm),
        grid=(n // tm,),
        in_specs=[pl.BlockSpec((tm, D_MODEL), lambda i: (i, 0)),
                  pl.BlockSpec((1, D_MODEL), lambda i: (0, 0)),
                  pl.BlockSpec((D_MODEL, LANES), lambda i: (0, 0)),
                  pl.BlockSpec((1, LANES), lambda i: (0, 0))],
        out_specs=[pl.BlockSpec((tm, LANES), lambda i: (i, 0)),
                   pl.BlockSpec((1, 8, tm), lambda i: (i, 0, 0)),
                   pl.BlockSpec((1, 1, LANES), lambda i: (i, 0, 0))],
        out_shape=[jax.ShapeDtypeStruct((n, LANES), F32),
                   jax.ShapeDtypeStruct((n // tm, 8, tm), F32),
                   jax.ShapeDtypeStruct((n // tm, 1, LANES), F32)],
        compiler_params=_cparams("parallel"),
        name="router",
    )(x2, g, rw_p, rb_p)


MOE_TILE = 512
RUN_ALIGN = 16
RUN_CHUNKS = (512, 256, 128, 64, 32, 16)
SORT_ROWS = -(-(MOE_TILE * TOP_K + N_EXPERTS * (RUN_ALIGN - 1)) // LANES) * LANES


def _run_copies(i, loc_ref, glob_ref, len_ref, make_copy, action):
    for e in range(N_EXPERTS):
        n = len_ref[i * N_EXPERTS + e]
        lo = loc_ref[i * N_EXPERTS + e]
        go = glob_ref[i * N_EXPERTS + e]
        for size in RUN_CHUNKS:
            done = n & ~(2 * size - 1)

            @pl.when((n & size) != 0)
            def _():
                action(make_copy(pl.multiple_of(lo + done, RUN_ALIGN), pl.multiple_of(go + done, RUN_ALIGN), size))


def _run_base(i, loc_ref, expert):
    base = jnp.zeros_like(expert)
    for e in range(N_EXPERTS):
        base = jnp.where(expert == e, loc_ref[i * N_EXPERTS + e], base)
    return base


def _dispatch_kernel(loc_ref, glob_ref, len_ref, x_ref, g_ref, metat_ref, xs_in_hbm, xs_hbm, sorted_ref, sem):
    del xs_in_hbm
    i = pl.program_id(0)
    h = _rms(x_ref[...], g_ref[...]).astype(BF16)
    rec = metat_ref[0]
    slot0 = _run_base(i, loc_ref, rec[R_E0:R_E0 + 1].astype(jnp.int32)) + rec[R_RANK0:R_RANK0 + 1].astype(jnp.int32)
    slot1 = _run_base(i, loc_ref, rec[R_E1:R_E1 + 1].astype(jnp.int32)) + rec[R_RANK1:R_RANK1 + 1].astype(jnp.int32)
    rows = lax.broadcasted_iota(jnp.int32, (SORT_ROWS, MOE_TILE), 0)
    perm = jnp.logical_or(rows == slot0, rows == slot1).astype(BF16)
    sorted_ref[...] = jnp.dot(perm, h, preferred_element_type=F32).astype(BF16)

    def make_copy(lo, go, size):
        return pltpu.make_async_copy(sorted_ref.at[pl.ds(lo, size), :], xs_hbm.at[pl.ds(go, size), :], sem)

    _run_copies(i, loc_ref, glob_ref, len_ref, make_copy, lambda c: c.start())
    _run_copies(i, loc_ref, glob_ref, len_ref, make_copy, lambda c: c.wait())


def _dispatch(x2, g, metat, loc, glob, plen, total):
    n = x2.shape[0]
    zeros = jnp.zeros((total, D_MODEL), BF16)
    return pl.pallas_call(
        _dispatch_kernel,
        grid_spec=pltpu.PrefetchScalarGridSpec(
            num_scalar_prefetch=3,
            grid=(n // MOE_TILE,),
            in_specs=[pl.BlockSpec((MOE_TILE, D_MODEL), lambda i, *_: (i, 0)),
                      pl.BlockSpec((1, D_MODEL), lambda i, *_: (0, 0)),
                      pl.BlockSpec((1, 8, MOE_TILE), lambda i, *_: (i, 0, 0)),
                      pl.BlockSpec(memory_space=pl.ANY)],
            out_specs=pl.BlockSpec(memory_space=pl.ANY),
            scratch_shapes=[pltpu.VMEM((SORT_ROWS, D_MODEL), BF16), pltpu.SemaphoreType.DMA(())]),
        out_shape=jax.ShapeDtypeStruct((total, D_MODEL), BF16),
        input_output_aliases={6: 0},
        compiler_params=_cparams("arbitrary"),
        name="moe_dispatch",
    )(loc, glob, plen, x2, g, metat, zeros)


MOE_CHUNK = 512


def _expert_kernel(blk_e_ref, n_used_ref, xs_ref, w1_ref, w3_ref, w2_ref, ys_ref, act_ref):
    j = pl.program_id(0)

    @pl.when(j < n_used_ref[0])
    def _():
        h = xs_ref[...]
        ffn = w1_ref.shape[2]
        for c0 in range(0, ffn, MOE_CHUNK):
            a = jnp.dot(h, w1_ref[0, :, c0:c0 + MOE_CHUNK], preferred_element_type=F32)
            b = jnp.dot(h, w3_ref[0, :, c0:c0 + MOE_CHUNK], preferred_element_type=F32)
            act_ref[:, c0:c0 + MOE_CHUNK] = (_silu(a) * b).astype(BF16)
        ys_ref[...] = jnp.dot(act_ref[...], w2_ref[0], preferred_element_type=F32).astype(BF16)

    @pl.when(j >= n_used_ref[0])
    def _():
        ys_ref[...] = jnp.zeros_like(ys_ref)


def _experts(xs, w1, w3, w2, blk_e, n_used):
    total = xs.shape[0]
    ffn = w1.shape[2]
    n_blocks = total // EXPERT_BLOCK
    return pl.pallas_call(
        _expert_kernel,
        grid_spec=pltpu.PrefetchScalarGridSpec(
            num_scalar_prefetch=2,
            grid=(n_blocks,),
            in_specs=[pl.BlockSpec((EXPERT_BLOCK, D_MODEL), lambda j, be, nu: (j, 0)),
                      _resident((1, D_MODEL, ffn), lambda j, be, nu: (be[j], 0, 0)),
                      _resident((1, D_MODEL, ffn), lambda j, be, nu: (be[j], 0, 0)),
                      _resident((1, ffn, D_MODEL), lambda j, be, nu: (be[j], 0, 0))],
            out_specs=pl.BlockSpec((EXPERT_BLOCK, D_MODEL), lambda j, be, nu: (j, 0)),
            scratch_shapes=[pltpu.VMEM((EXPERT_BLOCK, ffn), BF16)]),
        out_shape=jax.ShapeDtypeStruct((total, D_MODEL), BF16),
        compiler_params=_cparams("arbitrary"),
        name="moe_experts",
    )(blk_e, n_used, xs, w1, w3, w2)


def _combine_kernel(loc_ref, glob_ref, len_ref, x_ref, meta_ref, ys_hbm, o_ref, buf_ref, sem):
    i = pl.program_id(0)

    def make_copy(lo, go, size):
        return pltpu.make_async_copy(ys_hbm.at[pl.ds(go, size), :], buf_ref.at[pl.ds(lo, size), :], sem)

    _run_copies(i, loc_ref, glob_ref, len_ref, make_copy, lambda c: c.start())
    rec = meta_ref[...]
    slot0 = _run_base(i, loc_ref, rec[:, R_E0:R_E0 + 1].astype(jnp.int32)) + rec[:, R_RANK0:R_RANK0 + 1].astype(jnp.int32)
    slot1 = _run_base(i, loc_ref, rec[:, R_E1:R_E1 + 1].astype(jnp.int32)) + rec[:, R_RANK1:R_RANK1 + 1].astype(jnp.int32)
    cols = lax.broadcasted_iota(jnp.int32, (MOE_TILE, SORT_ROWS), 1)
    weight = jnp.where(cols == slot0, rec[:, R_G0:R_G0 + 1],
                       jnp.where(cols == slot1, rec[:, R_G1:R_G1 + 1], 0.0)).astype(BF16)
    _run_copies(i, loc_ref, glob_ref, len_ref, make_copy, lambda c: c.wait())
    filled = loc_ref[i * N_EXPERTS + N_EXPERTS - 1] + len_ref[i * N_EXPERTS + N_EXPERTS - 1]
    rows = lax.broadcasted_iota(jnp.int32, (SORT_ROWS, 1), 0)
    picked = jnp.where(rows < filled, buf_ref[...], jnp.zeros((), BF16))
    o_ref[...] = x_ref[...] + jnp.dot(weight, picked, preferred_element_type=F32)


def _combine(x2, meta, ys, loc, glob, plen):
    n = x2.shape[0]
    return pl.pallas_call(
        _combine_kernel,
        grid_spec=pltpu.PrefetchScalarGridSpec(
            num_scalar_prefetch=3,
            grid=(n // MOE_TILE,),
            in_specs=[pl.BlockSpec((MOE_TILE, D_MODEL), lambda i, *_: (i, 0)),
                      pl.BlockSpec((MOE_TILE, LANES), lambda i, *_: (i, 0)),
                      pl.BlockSpec(memory_space=pl.ANY)],
            out_specs=pl.BlockSpec((MOE_TILE, D_MODEL), lambda i, *_: (i, 0)),
            scratch_shapes=[pltpu.VMEM((SORT_ROWS, D_MODEL), BF16), pltpu.SemaphoreType.DMA(())]),
        out_shape=jax.ShapeDtypeStruct((n, D_MODEL), F32),
        compiler_params=_cparams("arbitrary"),
        name="moe_combine",
    )(loc, glob, plen, x2, meta, ys)


def _moe(x2, g, rw, rb, w1, w3, w2):
    n = x2.shape[0]
    n_tiles = n // MOE_TILE
    meta, metat, cnt = _router(x2, g, rw, rb, MOE_TILE)
    cnt = cnt[:, 0, :N_EXPERTS].astype(jnp.int32)
    plen = (cnt + RUN_ALIGN - 1) // RUN_ALIGN * RUN_ALIGN
    loc = jnp.cumsum(plen, axis=1) - plen
    padded = (jnp.sum(plen, axis=0) + EXPERT_BLOCK - 1) // EXPERT_BLOCK * EXPERT_BLOCK
    pend = jnp.cumsum(padded)
    glob = (pend - padded)[None, :] + jnp.cumsum(plen, axis=0) - plen
    n_blocks = -(-(n * TOP_K + n_tiles * N_EXPERTS * (RUN_ALIGN - 1)) // EXPERT_BLOCK) + N_EXPERTS
    blk_start = jnp.arange(n_blocks, dtype=jnp.int32) * EXPERT_BLOCK
    blk_e = jnp.minimum(jnp.sum(blk_start[:, None] >= pend[None, :], axis=1), N_EXPERTS - 1).astype(jnp.int32)
    n_used = (pend[-1] // EXPERT_BLOCK).astype(jnp.int32).reshape(1)
    loc, glob, plen = (a.reshape(-1).astype(jnp.int32) for a in (loc, glob, plen))
    xs = _dispatch(x2, g, metat, loc, glob, plen, n_blocks * EXPERT_BLOCK)
    ys = _experts(xs, w1, w3, w2, blk_e, n_used)
    return _combine(x2, meta, ys, loc, glob, plen)


def kernel(x, norm_mix_g, w_in, q_norm_g, k_norm_g, rel_bias, pool_w, pool_scale, w_branch_attn,
           w_branch_fourier, w_branch_pool, w_out, norm_ffn_g, ffn_w1, ffn_w3, ffn_w2,
           router_w, router_b, moe_w1, moe_w3, moe_w2):
    batch, seq, _ = x.shape
    depth = w_in.shape[0]
    n = batch * seq
    x2 = x.reshape(n, D_MODEL)
    biases = [_attn_bias(rel_bias, g) for g in range(len(ATTN_GROUPS))]
    dft = _dft_tables(seq)
    row = lambda v: v.reshape(1, -1).astype(F32)
    for layer in range(depth):
        *qkvs, uf, up, gates = _in_proj(x2, row(norm_mix_g[layer]), w_in[layer].astype(BF16),
                                        row(q_norm_g[layer]) * (HEAD_DIM ** -0.5), row(k_norm_g[layer]))
        os_, ls_ = [], []
        for g in range(len(ATTN_GROUPS)):
            o, lse = _attention_group(qkvs[g], biases[g], g, batch, seq)
            os_.append(o)
            ls_.append(lse)
        four = _fourier(uf, dft, batch, seq)
        pool = _pool(up, pool_w[layer].astype(BF16), row(pool_scale[layer]), batch, seq)
        dense = layer % 2 == 0
        j = layer // 2
        outs = _merge(x2, os_, ls_, four, pool, gates, w_branch_attn[layer].astype(BF16),
                      w_branch_fourier[layer].astype(BF16), w_branch_pool[layer].astype(BF16),
                      w_out[layer].astype(BF16), row(norm_ffn_g[layer]), with_h=dense)
        if dense:
            x2 = _dense_ffn(outs[0], outs[1], ffn_w1[j].astype(BF16), ffn_w3[j].astype(BF16),
                            ffn_w2[j].astype(BF16))
        else:
            x2 = _moe(outs[0], row(norm_ffn_g[layer]), router_w[j], router_b[j],
                      moe_w1[j].astype(BF16), moe_w3[j].astype(BF16), moe_w2[j].astype(BF16))
    return x2.reshape(batch, seq, D_MODEL)
```

```python
import functools
import math

import numpy as np
import jax
import jax.numpy as jnp
from jax import lax
from jax.experimental import pallas as pl
from jax.experimental.pallas import tpu as pltpu

F32 = jnp.float32
BF16 = jnp.bfloat16

D_MODEL = 1024
HEAD_DIM = 128
ATTN_GROUPS = ((128, 1), (512, 4), (2048, 16))
HEADS_PER_GROUP = 4
N_ATTN_HEADS = HEADS_PER_GROUP * len(ATTN_GROUPS)
ATTN_WIDTH = N_ATTN_HEADS * HEAD_DIM
GROUP_WIDTH = HEADS_PER_GROUP * HEAD_DIM
QKV_WIDTH = 3 * ATTN_WIDTH
GROUP_QKV = 3 * GROUP_WIDTH
N_BUCKETS = 32
REL_MAX_DISTANCE = 1024
FOURIER_GROUPS = 4
FOURIER_WIDTH = 512
POOL_WINDOWS = (2, 4, 8, 16)
POOL_WIDTH = 512
N_BRANCHES = 3
GATE_WIDTH = N_BRANCHES * D_MODEL
IN_COLS = QKV_WIDTH + FOURIER_WIDTH + POOL_WIDTH + GATE_WIDTH
N_EXPERTS = 8
TOP_K = 2
EXPERT_BLOCK = 512
RMS_EPS = 1e-6
NEG_INF = -1e30

LANES = 128
HALF_WINDOW = 64
Q_SUB = 128
K_WIN = Q_SUB + 2 * HALF_WINDOW
ATTN_ROWS = 1024
POOL_PAD = 8
DFT_TILE = 512
DFT_EXTRA = 8
VMEM_LIMIT = 56 * 1024 * 1024


def _cparams(*sem):
    return pltpu.CompilerParams(dimension_semantics=sem, vmem_limit_bytes=VMEM_LIMIT)


def _resident(shape, index_map):
    return pl.BlockSpec(shape, index_map, pipeline_mode=pl.Buffered(1))


def _rms(x, gain):
    return x * lax.rsqrt(jnp.mean(x * x, axis=-1, keepdims=True) + RMS_EPS) * gain


IN_CHUNK = 512


def _in_proj_kernel(x_ref, g_ref, w_ref, qg_ref, kg_ref, qkv0_ref, qkv1_ref, qkv2_ref, uf_ref, up_ref, gate_ref,
                    h_ref, slab_ref):
    tm = x_ref.shape[0]
    qkv_refs = (qkv0_ref, qkv1_ref, qkv2_ref)
    h_ref[...] = _rms(x_ref[...], g_ref[...]).astype(BF16)
    pieces = IN_CHUNK // LANES
    for c in range(IN_COLS // IN_CHUNK):
        col = c * IN_CHUNK
        acc = jnp.dot(h_ref[...], w_ref[:, col:col + IN_CHUNK], preferred_element_type=F32)
        slabs = slab_ref.at[c % 2]
        for hh in range(pieces):
            slabs[hh] = acc[:, hh * LANES:(hh + 1) * LANES]
        if col < QKV_WIDTH:
            kind, group = divmod(c, len(ATTN_GROUPS))
            dilation = ATTN_GROUPS[group][1]
            out_ref = qkv_refs[group]
            for hh in range(HEADS_PER_GROUP):
                lo = kind * GROUP_WIDTH + hh * HEAD_DIM
                for r in range(dilation):
                    a = slabs[hh] if dilation == 1 else slabs[hh, pl.ds(r, tm // dilation, stride=dilation), :]
                    if kind < 2:
                        a = _rms(a, qg_ref[...] if kind == 0 else kg_ref[...])
                    out_ref[:, r * GROUP_QKV + lo:r * GROUP_QKV + lo + HEAD_DIM] = a.astype(BF16)
        elif col < QKV_WIDTH + FOURIER_WIDTH + POOL_WIDTH:
            out_ref = uf_ref if col < QKV_WIDTH + FOURIER_WIDTH else up_ref
            for hh in range(pieces):
                out_ref[:, hh * LANES:(hh + 1) * LANES] = slabs[hh].astype(BF16)
        else:
            lo = col - (IN_COLS - GATE_WIDTH)
            for hh in range(pieces):
                gate = 1.0 / (1.0 + jnp.exp(-slabs[hh]))
                gate_ref[:, lo + hh * LANES:lo + (hh + 1) * LANES] = gate.astype(BF16)


def _in_proj(x2, g, w, qg, kg, tm=512):
    n = x2.shape[0]
    row = lambda i: (i, 0)
    fix = lambda i: (0, 0)
    return pl.pallas_call(
        _in_proj_kernel,
        grid=(n // tm,),
        in_specs=[pl.BlockSpec((tm, D_MODEL), row),
                  pl.BlockSpec((1, D_MODEL), fix),
                  _resident((D_MODEL, IN_COLS), fix),
                  pl.BlockSpec((1, HEAD_DIM), fix),
                  pl.BlockSpec((1, HEAD_DIM), fix)],
        out_specs=[pl.BlockSpec((tm // d, d * GROUP_QKV), row) for _, d in ATTN_GROUPS]
        + [pl.BlockSpec((tm, FOURIER_WIDTH), row),
           pl.BlockSpec((tm, POOL_WIDTH), row),
           pl.BlockSpec((tm, GATE_WIDTH), row)],
        out_shape=[jax.ShapeDtypeStruct((n // d, d * GROUP_QKV), BF16) for _, d in ATTN_GROUPS]
        + [jax.ShapeDtypeStruct((n, FOURIER_WIDTH), BF16),
           jax.ShapeDtypeStruct((n, POOL_WIDTH), BF16),
           jax.ShapeDtypeStruct((n, GATE_WIDTH), BF16)],
        scratch_shapes=[pltpu.VMEM((tm, D_MODEL), BF16),
                        pltpu.VMEM((2, IN_CHUNK // LANES, tm, LANES), F32)],
        compiler_params=_cparams("parallel"),
        name="in_proj",
    )(x2, g, w, qg, kg)


def _t5_bucket(rel):
    half = N_BUCKETS // 2
    ret = (rel > 0).astype(np.int64) * half
    n = np.abs(rel)
    max_exact = half // 2
    large = max_exact + (np.log(np.maximum(n, 1) / max_exact) / np.log(REL_MAX_DISTANCE / max_exact)
                         * (half - max_exact)).astype(np.int64)
    large = np.minimum(large, half - 1)
    return (ret + np.where(n < max_exact, n, large)).astype(np.int32)


def _attn_bias(rel_bias, group):
    dilation = ATTN_GROUPS[group][1]
    r = np.arange(Q_SUB)[:, None]
    c = np.arange(K_WIN)[None, :]
    heads = rel_bias[:, group * HEADS_PER_GROUP:(group + 1) * HEADS_PER_GROUP].astype(F32)
    off = np.stack([c - variant * HALF_WINDOW - r for variant in range(3)])
    valid = np.abs(off) <= HALF_WINDOW
    bucket = _t5_bucket(np.clip(off, -HALF_WINDOW, HALF_WINDOW) * dilation)
    onehot = (jnp.asarray(bucket)[..., None] == jnp.arange(N_BUCKETS)).astype(F32)
    b = jnp.einsum("vqkn,nh->vhqk", onehot, heads, precision=lax.Precision.HIGHEST)
    return jnp.where(valid[:, None], b, NEG_INF)


def _attn_kernel(*refs, seq, tq, res, packed):
    if packed:
        qkv_ref, bias_ref, o_ref, lse_ref, vaug_ref = refs
        q_ref = k_ref = v_ref = qkv_ref
        base = lambda rr, kind: rr * GROUP_QKV + kind * GROUP_WIDTH
    else:
        q_ref, k_ref, v_ref, bias_ref, o_ref, lse_ref, vaug_ref = refs
        base = lambda rr, kind: 0
    t = pl.program_id(2)

    @pl.when(t == 0)
    def _():
        for rr in range(res):
            for h in range(HEADS_PER_GROUP):
                c = (rr * HEADS_PER_GROUP + h) * 2 * HEAD_DIM
                vc = base(rr, 2) + h * HEAD_DIM
                vaug_ref[:, c:c + HEAD_DIM] = v_ref[0, :, vc:vc + HEAD_DIM]
                vaug_ref[:, c + HEAD_DIM:c + 2 * HEAD_DIM] = jnp.ones((seq, HEAD_DIM), BF16)

    lane = lax.broadcasted_iota(jnp.int32, (Q_SUB, LANES), 1)
    for rr in range(res):
        for s in range(tq // Q_SUB):
            rows = slice(s * Q_SUB, (s + 1) * Q_SUB)
            q0 = t * tq + s * Q_SUB
            start = pl.multiple_of(jnp.clip(q0 - HALF_WINDOW, 0, seq - K_WIN), HALF_WINDOW)
            variant = (q0 - start) // HALF_WINDOW
            lse_tile = jnp.zeros((Q_SUB, LANES), F32)
            for h in range(HEADS_PER_GROUP):
                qc = base(rr, 0) + h * HEAD_DIM
                kc = base(rr, 1) + h * HEAD_DIM
                c = (rr * HEADS_PER_GROUP + h) * 2 * HEAD_DIM
                q = q_ref[0, rows, qc:qc + HEAD_DIM]
                k = k_ref[0, pl.ds(start, K_WIN), kc:kc + HEAD_DIM]
                sc = lax.dot_general(q, k, (((1,), (1,)), ((), ())), preferred_element_type=F32)
                sc = sc + bias_ref[variant, h]
                m = jnp.max(sc, axis=-1, keepdims=True)
                p = jnp.exp(sc - m)
                pv = jnp.dot(p.astype(BF16), vaug_ref[pl.ds(start, K_WIN), c:c + 2 * HEAD_DIM],
                             preferred_element_type=F32)
                den = pv[:, HEAD_DIM:]
                oc = rr * GROUP_WIDTH + h * HEAD_DIM
                o_ref[0, rows, oc:oc + HEAD_DIM] = (pv[:, :HEAD_DIM] / den).astype(BF16)
                lse_tile = jnp.where(lane == h, m + jnp.log(den), lse_tile)
            lse_ref[0, rows, rr * LANES:(rr + 1) * LANES] = lse_tile


def _attention_group(qkv, bias, group, batch, seq_full):
    dilation = ATTN_GROUPS[group][1]
    seq = seq_full // dilation
    tq = min(ATTN_ROWS, seq)
    qkv_v = qkv.reshape(batch, seq, dilation * GROUP_QKV)
    packed = tq == seq
    res = max(1, ATTN_ROWS // seq) if packed else 1
    if packed:
        qkv_specs = [pl.BlockSpec((1, seq, res * GROUP_QKV), lambda b, r, t: (b, 0, r))]
    else:
        qkv_specs = [pl.BlockSpec((1, tq, GROUP_WIDTH), lambda b, r, t: (b, t, 3 * r)),
                     pl.BlockSpec((1, seq, GROUP_WIDTH), lambda b, r, t: (b, 0, 3 * r + 1)),
                     pl.BlockSpec((1, seq, GROUP_WIDTH), lambda b, r, t: (b, 0, 3 * r + 2))]
    o, lse = pl.pallas_call(
        functools.partial(_attn_kernel, seq=seq, tq=tq, res=res, packed=packed),
        grid=(batch, dilation // res, seq // tq),
        in_specs=qkv_specs + [pl.BlockSpec((3, HEADS_PER_GROUP, Q_SUB, K_WIN), lambda b, r, t: (0, 0, 0, 0))],
        out_specs=[pl.BlockSpec((1, tq, res * GROUP_WIDTH), lambda b, r, t: (b, t, r)),
                   pl.BlockSpec((1, tq, res * LANES), lambda b, r, t: (b, t, r))],
        out_shape=[jax.ShapeDtypeStruct((batch, seq, dilation * GROUP_WIDTH), BF16),
                   jax.ShapeDtypeStruct((batch, seq, dilation * LANES), F32)],
        scratch_shapes=[pltpu.VMEM((seq, res * 2 * GROUP_WIDTH), BF16)],
        compiler_params=_cparams("parallel", "parallel", "arbitrary"),
        name=f"attn_g{group}",
    )(*([qkv_v] * len(qkv_specs)), bias)
    return o.reshape(batch * seq, dilation * GROUP_WIDTH), lse.reshape(batch * seq, dilation * LANES)


def _dft_tables(seq):
    c = jnp.arange(HEAD_DIM, dtype=jnp.int32)
    ang_c = ((c[:, None] * c[None, :]) % HEAD_DIM).astype(F32) * (2.0 * math.pi / HEAD_DIM)
    chan = jnp.concatenate([jnp.cos(ang_c), jnp.sin(ang_c)], axis=1) * (HEAD_DIM ** -0.5)
    tiles = seq // 2 // DFT_TILE
    k = (jnp.arange(tiles, dtype=jnp.int32)[:, None] * DFT_TILE
         + jnp.arange(DFT_TILE + DFT_EXTRA, dtype=jnp.int32)[None, :])
    s = jnp.arange(seq, dtype=jnp.int32)
    ang = ((k[..., None] * s) % seq).astype(F32) * (2.0 * math.pi / seq)
    cos_t = jnp.cos(ang) * (seq ** -0.5)
    sin_t = jnp.sin(ang) * (seq ** -0.5)
    j = np.arange(DFT_TILE)[:, None]
    flip = (np.arange(DFT_TILE + DFT_EXTRA)[None, :] == DFT_TILE - j).astype(np.float32)
    return chan.astype(BF16), cos_t.astype(BF16), sin_t.astype(BF16), jnp.asarray(flip, BF16)


def _fourier_kernel(u_ref, chan_ref, cos_ref, sin_ref, flip_ref, o_ref, y_ref, *, seq):
    m = pl.program_id(1)

    @pl.when(m == 0)
    def _():
        rows = 512
        for r0 in range(0, seq, rows):
            for g in range(FOURIER_GROUPS):
                y = jnp.dot(u_ref[0, r0:r0 + rows, g * HEAD_DIM:(g + 1) * HEAD_DIM], chan_ref[...],
                            preferred_element_type=F32)
                y_ref[r0:r0 + rows, g * HEAD_DIM:(g + 1) * HEAD_DIM] = y[:, :HEAD_DIM].astype(BF16)
                y_ref[seq + r0:seq + r0 + rows, g * HEAD_DIM:(g + 1) * HEAD_DIM] = y[:, HEAD_DIM:].astype(BF16)

    even = jnp.dot(cos_ref[0], y_ref[0:seq, :], preferred_element_type=F32)
    odd = jnp.dot(sin_ref[0], y_ref[seq:2 * seq, :], preferred_element_type=F32)
    lo = pl.multiple_of(m * DFT_TILE, DFT_TILE)
    o_ref[0, pl.ds(lo, DFT_TILE), :] = (even - odd)[:DFT_TILE].astype(BF16)
    mirrored = jnp.dot(flip_ref[...], (even + odd).astype(BF16), preferred_element_type=F32)
    hi = pl.multiple_of(seq - lo - DFT_TILE, DFT_TILE)
    o_ref[0, pl.ds(hi, DFT_TILE), :] = mirrored.astype(BF16)


def _fourier(uf, tables, batch, seq):
    chan, cos_t, sin_t, flip = tables
    u3 = uf.reshape(batch, seq, FOURIER_WIDTH)
    rows = DFT_TILE + DFT_EXTRA
    out = pl.pallas_call(
        functools.partial(_fourier_kernel, seq=seq),
        grid=(batch, seq // 2 // DFT_TILE),
        in_specs=[pl.BlockSpec((1, seq, FOURIER_WIDTH), lambda b, m: (b, 0, 0)),
                  pl.BlockSpec((HEAD_DIM, 2 * HEAD_DIM), lambda b, m: (0, 0)),
                  pl.BlockSpec((1, rows, seq), lambda b, m: (m, 0, 0)),
                  pl.BlockSpec((1, rows, seq), lambda b, m: (m, 0, 0)),
                  pl.BlockSpec((DFT_TILE, rows), lambda b, m: (0, 0))],
        out_specs=pl.BlockSpec((1, seq, FOURIER_WIDTH), lambda b, m: (b, 0, 0)),
        out_shape=jax.ShapeDtypeStruct((batch, seq, FOURIER_WIDTH), BF16),
        scratch_shapes=[pltpu.VMEM((2 * seq, FOURIER_WIDTH), BF16)],
        compiler_params=_cparams("parallel", "arbitrary"),
        name="fourier",
    )(u3, chan, cos_t, sin_t, flip)
    return out.reshape(batch * seq, FOURIER_WIDTH)


def _pool_kernel(u_ref, w_ref, scale_ref, o_ref, pad_ref, *, seq):
    rows = 512
    zeros = jnp.zeros((POOL_PAD, HEAD_DIM), F32)
    pad_ref[0:POOL_PAD, :] = zeros
    pad_ref[POOL_PAD + seq:2 * POOL_PAD + seq, :] = zeros
    for g, window in enumerate(POOL_WINDOWS):
        cols = slice(g * HEAD_DIM, (g + 1) * HEAD_DIM)
        half = window // 2
        pad_ref[POOL_PAD:POOL_PAD + seq, :] = u_ref[0, :, cols].astype(F32)
        for r0 in range(0, seq, rows):
            base = POOL_PAD + r0
            acc = pad_ref[base - half:base - half + rows, :]
            for off in range(-half + 1, half):
                acc = acc + pad_ref[base + off:base + off + rows, :]
            pos = r0 + lax.broadcasted_iota(jnp.int32, (rows, 1), 0)
            cnt = jnp.minimum(pos + half - 1, seq - 1) - jnp.maximum(pos - half, 0) + 1
            d = acc / cnt.astype(F32) - pad_ref[base:base + rows, :]
            y = jnp.dot(d.astype(BF16), w_ref[g], preferred_element_type=F32) * scale_ref[:, cols]
            o_ref[0, r0:r0 + rows, cols] = y.astype(BF16)


def _pool(up, w, scale, batch, seq):
    u3 = up.reshape(batch, seq, POOL_WIDTH)
    out = pl.pallas_call(
        functools.partial(_pool_kernel, seq=seq),
        grid=(batch,),
        in_specs=[pl.BlockSpec((1, seq, POOL_WIDTH), lambda b: (b, 0, 0)),
                  pl.BlockSpec((len(POOL_WINDOWS), HEAD_DIM, HEAD_DIM), lambda b: (0, 0, 0)),
                  pl.BlockSpec((1, POOL_WIDTH), lambda b: (0, 0))],
        out_specs=pl.BlockSpec((1, seq, POOL_WIDTH), lambda b: (b, 0, 0)),
        out_shape=jax.ShapeDtypeStruct((batch, seq, POOL_WIDTH), BF16),
        scratch_shapes=[pltpu.VMEM((seq + 2 * POOL_PAD, HEAD_DIM), F32)],
        compiler_params=_cparams("parallel"),
        name="pool",
    )(u3, w, scale)
    return out.reshape(batch * seq, POOL_WIDTH)


def _merge_kernel(x_ref, o1_ref, o2_ref, o3_ref, l1_ref, l2_ref, l3_ref, four_ref, pool_ref, gate_ref,
                  wba_ref, wbf_ref, wbp_ref, wout_ref, gffn_ref, xo_ref, *rest):
    h_refs, (oslab_ref, lslab_ref, attn_ref, acc_ref) = rest[:-4], rest[-4:]
    tm = x_ref.shape[0]
    o_refs = (o1_ref, o2_ref, o3_ref)
    l_refs = (l1_ref, l2_ref, l3_ref)
    f = jnp.dot(four_ref[...], wbf_ref[...], preferred_element_type=F32)
    p = jnp.dot(pool_ref[...], wbp_ref[...], preferred_element_type=F32)
    acc_ref[...] = (gate_ref[:, D_MODEL:2 * D_MODEL].astype(F32) * f
                    + gate_ref[:, 2 * D_MODEL:3 * D_MODEL].astype(F32) * p)
    for g, (_, d) in enumerate(ATTN_GROUPS):
        if d == 1:
            continue
        for r in range(d):
            rows = pl.ds(r, tm // d, stride=d)
            lslab_ref[g - 1, rows, :] = l_refs[g][:, r * LANES:(r + 1) * LANES]
            for j in range(HEADS_PER_GROUP):
                lo = r * GROUP_WIDTH + j * HEAD_DIM
                oslab_ref[g - 1, j, rows, :] = o_refs[g][:, lo:lo + HEAD_DIM].astype(F32)
    for j in range(HEADS_PER_GROUP):
        cols = slice(j * HEAD_DIM, (j + 1) * HEAD_DIM)
        lses = [l1_ref[:, j:j + 1], lslab_ref[0, :, j:j + 1], lslab_ref[1, :, j:j + 1]]
        outs = [o1_ref[:, cols].astype(F32), oslab_ref[0, j], oslab_ref[1, j]]
        top = jnp.maximum(jnp.maximum(lses[0], lses[1]), lses[2])
        es = [jnp.exp(l - top) for l in lses]
        num = sum(e * o for e, o in zip(es, outs))
        attn_ref[:, cols] = (num / (es[0] + es[1] + es[2])).astype(BF16)
    a = jnp.dot(attn_ref[...], wba_ref[...], preferred_element_type=F32)
    merged = gate_ref[:, 0:D_MODEL].astype(F32) * a + acc_ref[...]
    xn = x_ref[...] + jnp.dot(merged.astype(BF16), wout_ref[...], preferred_element_type=F32)
    xo_ref[...] = xn
    if h_refs:
        h_refs[0][...] = _rms(xn, gffn_ref[...]).astype(BF16)


def _merge(x2, os_, ls_, four, pool, gates, wba, wbf, wbp, wout, gffn, with_h, tm=512):
    n = x2.shape[0]
    row = lambda i: (i, 0)
    fix = lambda i: (0, 0)
    out_specs = [pl.BlockSpec((tm, D_MODEL), row)]
    out_shape = [jax.ShapeDtypeStruct((n, D_MODEL), F32)]
    if with_h:
        out_specs.append(pl.BlockSpec((tm, D_MODEL), row))
        out_shape.append(jax.ShapeDtypeStruct((n, D_MODEL), BF16))
    return pl.pallas_call(
        _merge_kernel,
        grid=(n // tm,),
        in_specs=[pl.BlockSpec((tm, D_MODEL), row)]
        + [pl.BlockSpec((tm // d, d * GROUP_WIDTH), row) for _, d in ATTN_GROUPS]
        + [pl.BlockSpec((tm // d, d * LANES), row) for _, d in ATTN_GROUPS]
        + [pl.BlockSpec((tm, FOURIER_WIDTH), row), pl.BlockSpec((tm, POOL_WIDTH), row),
           pl.BlockSpec((tm, GATE_WIDTH), row)]
        + [pl.BlockSpec((GROUP_WIDTH, D_MODEL), fix)] * 3
        + [pl.BlockSpec((D_MODEL, D_MODEL), fix), pl.BlockSpec((1, D_MODEL), fix)],
        out_specs=out_specs,
        out_shape=out_shape,
        scratch_shapes=[pltpu.VMEM((len(ATTN_GROUPS) - 1, HEADS_PER_GROUP, tm, HEAD_DIM), F32),
                        pltpu.VMEM((len(ATTN_GROUPS) - 1, tm, LANES), F32),
                        pltpu.VMEM((tm, GROUP_WIDTH), BF16),
                        pltpu.VMEM((tm, D_MODEL), F32)],
        compiler_params=_cparams("parallel"),
        name="merge",
    )(x2, *os_, *ls_, four, pool, gates, wba, wbf, wbp, wout, gffn)


FFN_CHUNK = 256


def _silu(a):
    return a * (1.0 / (1.0 + jnp.exp(-a)))


def _ffn_kernel(x_ref, h_ref, w1_ref, w3_ref, w2_ref, o_ref, g_ref):
    h = h_ref[...]
    ffn = w1_ref.shape[1]
    for c0 in range(0, ffn, FFN_CHUNK):
        a = jnp.dot(h, w1_ref[:, c0:c0 + FFN_CHUNK], preferred_element_type=F32)
        b = jnp.dot(h, w3_ref[:, c0:c0 + FFN_CHUNK], preferred_element_type=F32)
        g_ref[:, c0:c0 + FFN_CHUNK] = (_silu(a) * b).astype(BF16)
    o_ref[...] = x_ref[...] + jnp.dot(g_ref[...], w2_ref[...], preferred_element_type=F32)


def _dense_ffn(x2, h, w1, w3, w2, tm=512):
    n = x2.shape[0]
    ffn = w1.shape[1]
    row = lambda i: (i, 0)
    fix = lambda i: (0, 0)
    return pl.pallas_call(
        _ffn_kernel,
        grid=(n // tm,),
        in_specs=[pl.BlockSpec((tm, D_MODEL), row), pl.BlockSpec((tm, D_MODEL), row),
                  _resident((D_MODEL, ffn), fix), _resident((D_MODEL, ffn), fix),
                  _resident((ffn, D_MODEL), fix)],
        out_specs=pl.BlockSpec((tm, D_MODEL), row),
        out_shape=jax.ShapeDtypeStruct((n, D_MODEL), F32),
        scratch_shapes=[pltpu.VMEM((tm, ffn), BF16)],
        compiler_params=_cparams("parallel"),
        name="dense_ffn",
    )(x2, h, w1, w3, w2)


R_E0, R_E1, R_G0, R_G1, R_RANK0, R_RANK1 = 0, 1, 2, 3, 4, 5


def _router_kernel(x_ref, g_ref, rw_ref, rb_ref, meta_ref, metat_ref, cnt_ref, *, tm):
    h = _rms(x_ref[...], g_ref[...])
    logits = jnp.dot(h.astype(BF16), rw_ref[...], preferred_element_type=F32) + rb_ref[...]
    lane = lax.broadcasted_iota(jnp.int32, (tm, LANES), 1)
    logits = jnp.where(lane < N_EXPERTS, logits, -jnp.inf)
    v0 = jnp.max(logits, axis=-1, keepdims=True)
    e0 = jnp.min(jnp.where(logits == v0, lane, LANES), axis=-1, keepdims=True)
    rest = jnp.where(lane == e0, -jnp.inf, logits)
    v1 = jnp.max(rest, axis=-1, keepdims=True)
    e1 = jnp.min(jnp.where(rest == v1, lane, LANES), axis=-1, keepdims=True)
    t = jnp.exp(v1 - v0)
    gate0 = 1.0 / (1.0 + t)
    gate1 = t / (1.0 + t)

    picked = jnp.logical_or(lane == e0, lane == e1)
    tri = (lax.broadcasted_iota(jnp.int32, (tm, tm), 1) < lax.broadcasted_iota(jnp.int32, (tm, tm), 0))
    before = jnp.dot(tri.astype(BF16), picked.astype(BF16), preferred_element_type=F32)
    rank0 = jnp.sum(jnp.where(lane == e0, before, 0.0), axis=-1, keepdims=True)
    rank1 = jnp.sum(jnp.where(lane == e1, before, 0.0), axis=-1, keepdims=True)

    meta = jnp.zeros((tm, LANES), F32)
    for ln, val in ((R_E0, e0.astype(F32)), (R_E1, e1.astype(F32)), (R_G0, gate0), (R_G1, gate1),
                    (R_RANK0, rank0), (R_RANK1, rank1)):
        meta = jnp.where(lane == ln, val, meta)
    meta_ref[...] = meta
    metat_ref[0] = jnp.transpose(meta)[:8]
    cnt_ref[0] = jnp.sum(picked.astype(F32), axis=0, keepdims=True)


def _router(x2, g, rw, rb, tm):
    n = x2.shape[0]
    rw_p = jnp.zeros((D_MODEL, LANES), BF16).at[:, :N_EXPERTS].set(rw.astype(BF16))
    rb_p = jnp.zeros((1, LANES), F32).at[0, :N_EXPERTS].set(rb.astype(F32))
    return pl.pallas_call(
        functools.partial(_router_kernel, tm=tm),
        grid=(n // tm,),
        in_specs=[pl.BlockSpec((tm, D_MODEL), lambda i: (i, 0)),
                  pl.BlockSpec((1, D_MODEL), lambda i: (0, 0)),
                  pl.BlockSpec((D_MODEL, LANES), lambda i: (0, 0)),
                  pl.BlockSpec((1, LANES), lambda i: (0, 0))],
        out_specs=[pl.BlockSpec((tm, LANES), lambda i: (i, 0)),
                   pl.BlockSpec((1, 8, tm), lambda i: (i, 0, 0)),
                   pl.BlockSpec((1, 1, LANES), lambda i: (i, 0, 0))],
        out_shape=[jax.ShapeDtypeStruct((n, LANES), F32),
                   jax.ShapeDtypeStruct((n // tm, 8, tm), F32),
                   jax.ShapeDtypeStruct((n // tm, 1, LANES), F32)],
        compiler_params=_cparams("parallel"),
        name="router",
    )(x2, g, rw_p, rb_p)


MOE_TILE = 512
RUN_ALIGN = 16
RUN_CHUNKS = (512, 256, 128, 64, 32, 16)
SORT_ROWS = -(-(MOE_TILE * TOP_K + N_EXPERTS * (RUN_ALIGN - 1)) // LANES) * LANES


def _run_copies(i, loc_ref, glob_ref, len_ref, make_copy, action):
    for e in range(N_EXPERTS):
        n = len_ref[i * N_EXPERTS + e]
        lo = loc_ref[i * N_EXPERTS + e]
        go = glob_ref[i * N_EXPERTS + e]
        for size in RUN_CHUNKS:
            done = n & ~(2 * size - 1)

            @pl.when((n & size) != 0)
            def _():
                action(make_copy(pl.multiple_of(lo + done, RUN_ALIGN), pl.multiple_of(go + done, RUN_ALIGN), size))


def _run_base(i, loc_ref, expert):
    base = jnp.zeros_like(expert)
    for e in range(N_EXPERTS):
        base = jnp.where(expert == e, loc_ref[i * N_EXPERTS + e], base)
    return base


def _dispatch_kernel(loc_ref, glob_ref, len_ref, x_ref, g_ref, metat_ref, xs_in_hbm, xs_hbm, sorted_ref, sem):
    del xs_in_hbm
    i = pl.program_id(0)
    h = _rms(x_ref[...], g_ref[...]).astype(BF16)
    rec = metat_ref[0]
    slot0 = _run_base(i, loc_ref, rec[R_E0:R_E0 + 1].astype(jnp.int32)) + rec[R_RANK0:R_RANK0 + 1].astype(jnp.int32)
    slot1 = _run_base(i, loc_ref, rec[R_E1:R_E1 + 1].astype(jnp.int32)) + rec[R_RANK1:R_RANK1 + 1].astype(jnp.int32)
    rows = lax.broadcasted_iota(jnp.int32, (SORT_ROWS, MOE_TILE), 0)
    perm = jnp.logical_or(rows == slot0, rows == slot1).astype(BF16)
    sorted_ref[...] = jnp.dot(perm, h, preferred_element_type=F32).astype(BF16)

    def make_copy(lo, go, size):
        return pltpu.make_async_copy(sorted_ref.at[pl.ds(lo, size), :], xs_hbm.at[pl.ds(go, size), :], sem)

    _run_copies(i, loc_ref, glob_ref, len_ref, make_copy, lambda c: c.start())
    _run_copies(i, loc_ref, glob_ref, len_ref, make_copy, lambda c: c.wait())


def _dispatch(x2, g, metat, loc, glob, plen, total):
    n = x2.shape[0]
    zeros = jnp.zeros((total, D_MODEL), BF16)
    return pl.pallas_call(
        _dispatch_kernel,
        grid_spec=pltpu.PrefetchScalarGridSpec(
            num_scalar_prefetch=3,
            grid=(n // MOE_TILE,),
            in_specs=[pl.BlockSpec((MOE_TILE, D_MODEL), lambda i, *_: (i, 0)),
                      pl.BlockSpec((1, D_MODEL), lambda i, *_: (0, 0)),
                      pl.BlockSpec((1, 8, MOE_TILE), lambda i, *_: (i, 0, 0)),
                      pl.BlockSpec(memory_space=pl.ANY)],
            out_specs=pl.BlockSpec(memory_space=pl.ANY),
            scratch_shapes=[pltpu.VMEM((SORT_ROWS, D_MODEL), BF16), pltpu.SemaphoreType.DMA(())]),
        out_shape=jax.ShapeDtypeStruct((total, D_MODEL), BF16),
        input_output_aliases={6: 0},
        compiler_params=_cparams("arbitrary"),
        name="moe_dispatch",
    )(loc, glob, plen, x2, g, metat, zeros)


MOE_CHUNK = 512


def _expert_kernel(blk_e_ref, n_used_ref, xs_ref, w1_ref, w3_ref, w2_ref, ys_ref, act_ref):
    j = pl.program_id(0)

    @pl.when(j < n_used_ref[0])
    def _():
        h = xs_ref[...]
        ffn = w1_ref.shape[2]
        for c0 in range(0, ffn, MOE_CHUNK):
            a = jnp.dot(h, w1_ref[0, :, c0:c0 + MOE_CHUNK], preferred_element_type=F32)
            b = jnp.dot(h, w3_ref[0, :, c0:c0 + MOE_CHUNK], preferred_element_type=F32)
            act_ref[:, c0:c0 + MOE_CHUNK] = (_silu(a) * b).astype(BF16)
        ys_ref[...] = jnp.dot(act_ref[...], w2_ref[0], preferred_element_type=F32).astype(BF16)

    @pl.when(j >= n_used_ref[0])
    def _():
        ys_ref[...] = jnp.zeros_like(ys_ref)


def _experts(xs, w1, w3, w2, blk_e, n_used):
    total = xs.shape[0]
    ffn = w1.shape[2]
    n_blocks = total // EXPERT_BLOCK
    return pl.pallas_call(
        _expert_kernel,
        grid_spec=pltpu.PrefetchScalarGridSpec(
            num_scalar_prefetch=2,
            grid=(n_blocks,),
            in_specs=[pl.BlockSpec((EXPERT_BLOCK, D_MODEL), lambda j, be, nu: (j, 0)),
                      _resident((1, D_MODEL, ffn), lambda j, be, nu: (be[j], 0, 0)),
                      _resident((1, D_MODEL, ffn), lambda j, be, nu: (be[j], 0, 0)),
                      _resident((1, ffn, D_MODEL), lambda j, be, nu: (be[j], 0, 0))],
            out_specs=pl.BlockSpec((EXPERT_BLOCK, D_MODEL), lambda j, be, nu: (j, 0)),
            scratch_shapes=[pltpu.VMEM((EXPERT_BLOCK, ffn), BF16)]),
        out_shape=jax.ShapeDtypeStruct((total, D_MODEL), BF16),
        compiler_params=_cparams("arbitrary"),
        name="moe_experts",
    )(blk_e, n_used, xs, w1, w3, w2)


def _combine_kernel(loc_ref, glob_ref, len_ref, x_ref, meta_ref, ys_hbm, o_ref, buf_ref, sem):
    i = pl.program_id(0)

    def make_copy(lo, go, size):
        return pltpu.make_async_copy(ys_hbm.at[pl.ds(go, size), :], buf_ref.at[pl.ds(lo, size), :], sem)

    _run_copies(i, loc_ref, glob_ref, len_ref, make_copy, lambda c: c.start())
    rec = meta_ref[...]
    slot0 = _run_base(i, loc_ref, rec[:, R_E0:R_E0 + 1].astype(jnp.int32)) + rec[:, R_RANK0:R_RANK0 + 1].astype(jnp.int32)
    slot1 = _run_base(i, loc_ref, rec[:, R_E1:R_E1 + 1].astype(jnp.int32)) + rec[:, R_RANK1:R_RANK1 + 1].astype(jnp.int32)
    cols = lax.broadcasted_iota(jnp.int32, (MOE_TILE, SORT_ROWS), 1)
    weight = jnp.where(cols == slot0, rec[:, R_G0:R_G0 + 1],
                       jnp.where(cols == slot1, rec[:, R_G1:R_G1 + 1], 0.0)).astype(BF16)
    _run_copies(i, loc_ref, glob_ref, len_ref, make_copy, lambda c: c.wait())
    filled = loc_ref[i * N_EXPERTS + N_EXPERTS - 1] + len_ref[i * N_EXPERTS + N_EXPERTS - 1]
    rows = lax.broadcasted_iota(jnp.int32, (SORT_ROWS, 1), 0)
    picked = jnp.where(rows < filled, buf_ref[...], jnp.zeros((), BF16))
    o_ref[...] = x_ref[...] + jnp.dot(weight, picked, preferred_element_type=F32)


def _combine(x2, meta, ys, loc, glob, plen):
    n = x2.shape[0]
    return pl.pallas_call(
        _combine_kernel,
        grid_spec=pltpu.PrefetchScalarGridSpec(
            num_scalar_prefetch=3,
            grid=(n // MOE_TILE,),
            in_specs=[pl.BlockSpec((MOE_TILE, D_MODEL), lambda i, *_: (i, 0)),
                      pl.BlockSpec((MOE_TILE, LANES), lambda i, *_: (i, 0)),
                      pl.BlockSpec(memory_space=pl.ANY)],
            out_specs=pl.BlockSpec((MOE_TILE, D_MODEL), lambda i, *_: (i, 0)),
            scratch_shapes=[pltpu.VMEM((SORT_ROWS, D_MODEL), BF16), pltpu.SemaphoreType.DMA(())]),
        out_shape=jax.ShapeDtypeStruct((n, D_MODEL), F32),
        compiler_params=_cparams("arbitrary"),
        name="moe_combine",
    )(loc, glob, plen, x2, meta, ys)


def _moe(x2, g, rw, rb, w1, w3, w2):
    n = x2.shape[0]
    n_tiles = n // MOE_TILE
    meta, metat, cnt = _router(x2, g, rw, rb, MOE_TILE)
    cnt = cnt[:, 0, :N_EXPERTS].astype(jnp.int32)
    plen = (cnt + RUN_ALIGN - 1) // RUN_ALIGN * RUN_ALIGN
    loc = jnp.cumsum(plen, axis=1) - plen
    padded = (jnp.sum(plen, axis=0) + EXPERT_BLOCK - 1) // EXPERT_BLOCK * EXPERT_BLOCK
    pend = jnp.cumsum(padded)
    glob = (pend - padded)[None, :] + jnp.cumsum(plen, axis=0) - plen
    n_blocks = -(-(n * TOP_K + n_tiles * N_EXPERTS * (RUN_ALIGN - 1)) // EXPERT_BLOCK) + N_EXPERTS
    blk_start = jnp.arange(n_blocks, dtype=jnp.int32) * EXPERT_BLOCK
    blk_e = jnp.minimum(jnp.sum(blk_start[:, None] >= pend[None, :], axis=1), N_EXPERTS - 1).astype(jnp.int32)
    n_used = (pend[-1] // EXPERT_BLOCK).astype(jnp.int32).reshape(1)
    loc, glob, plen = (a.reshape(-1).astype(jnp.int32) for a in (loc, glob, plen))
    xs = _dispatch(x2, g, metat, loc, glob, plen, n_blocks * EXPERT_BLOCK)
    ys = _experts(xs, w1, w3, w2, blk_e, n_used)
    return _combine(x2, meta, ys, loc, glob, plen)


def kernel(x, norm_mix_g, w_in, q_norm_g, k_norm_g, rel_bias, pool_w, pool_scale, w_branch_attn,
           w_branch_fourier, w_branch_pool, w_out, norm_ffn_g, ffn_w1, ffn_w3, ffn_w2,
           router_w, router_b, moe_w1, moe_w3, moe_w2):
    batch, seq, _ = x.shape
    depth = w_in.shape[0]
    n = batch * seq
    x2 = x.reshape(n, D_MODEL)
    biases = [_attn_bias(rel_bias, g) for g in range(len(ATTN_GROUPS))]
    dft = _dft_tables(seq)
    row = lambda v: v.reshape(1, -1).astype(F32)
    for layer in range(depth):
        *qkvs, uf, up, gates = _in_proj(x2, row(norm_mix_g[layer]), w_in[layer].astype(BF16),
                                        row(q_norm_g[layer]) * (HEAD_DIM ** -0.5), row(k_norm_g[layer]))
        os_, ls_ = [], []
        for g in range(len(ATTN_GROUPS)):
            o, lse = _attention_group(qkvs[g], biases[g], g, batch, seq)
            os_.append(o)
            ls_.append(lse)
        four = _fourier(uf, dft, batch, seq)
        pool = _pool(up, pool_w[layer].astype(BF16), row(pool_scale[layer]), batch, seq)
        dense = layer % 2 == 0
        j = layer // 2
        outs = _merge(x2, os_, ls_, four, pool, gates, w_branch_attn[layer].astype(BF16),
                      w_branch_fourier[layer].astype(BF16), w_branch_pool[layer].astype(BF16),
                      w_out[layer].astype(BF16), row(norm_ffn_g[layer]), with_h=dense)
        if dense:
            x2 = _dense_ffn(outs[0], outs[1], ffn_w1[j].astype(BF16), ffn_w3[j].astype(BF16),
                            ffn_w2[j].astype(BF16))
        else:
            x2 = _moe(outs[0], row(norm_ffn_g[layer]), router_w[j], router_b[j],
                      moe_w1[j].astype(BF16), moe_w3[j].astype(BF16), moe_w2[j].astype(BF16))
    return x2.reshape(batch, seq, D_MODEL)
```

```python
import functools
import math

import numpy as np
import jax
import jax.numpy as jnp
from jax import lax
from jax.experimental import pallas as pl
from jax.experimental.pallas import tpu as pltpu

F32 = jnp.float32
BF16 = jnp.bfloat16

D_MODEL = 1024
HEAD_DIM = 128
ATTN_GROUPS = ((128, 1), (512, 4), (2048, 16))
HEADS_PER_GROUP = 4
N_ATTN_HEADS = HEADS_PER_GROUP * len(ATTN_GROUPS)
ATTN_WIDTH = N_ATTN_HEADS * HEAD_DIM
GROUP_WIDTH = HEADS_PER_GROUP * HEAD_DIM
QKV_WIDTH = 3 * ATTN_WIDTH
GROUP_QKV = 3 * GROUP_WIDTH
N_BUCKETS = 32
REL_MAX_DISTANCE = 1024
FOURIER_GROUPS = 4
FOURIER_WIDTH = 512
POOL_WINDOWS = (2, 4, 8, 16)
POOL_WIDTH = 512
N_BRANCHES = 3
GATE_WIDTH = N_BRANCHES * D_MODEL
IN_COLS = QKV_WIDTH + FOURIER_WIDTH + POOL_WIDTH + GATE_WIDTH
N_EXPERTS = 8
TOP_K = 2
EXPERT_BLOCK = 512
RMS_EPS = 1e-6
NEG_INF = -1e30

LANES = 128
HALF_WINDOW = 64
Q_SUB = 128
K_WIN = Q_SUB + 2 * HALF_WINDOW
ATTN_ROWS = 1024
POOL_PAD = 8
DFT_TILE = 512
DFT_EXTRA = 8
VMEM_LIMIT = 56 * 1024 * 1024


def _cparams(*sem):
    return pltpu.CompilerParams(dimension_semantics=sem, vmem_limit_bytes=VMEM_LIMIT)


def _resident(shape, index_map):
    return pl.BlockSpec(shape, index_map, pipeline_mode=pl.Buffered(1))


def _layer_block(shape, layer, resident=False):
    index_map = lambda *_: (layer,) + (0,) * len(shape)
    make = _resident if resident else pl.BlockSpec
    return make((None,) + tuple(shape), index_map)


def _rms(x, gain):
    return x * lax.rsqrt(jnp.mean(x * x, axis=-1, keepdims=True) + RMS_EPS) * gain


IN_CHUNK = 512
EPI_ROWS = 128


def _in_proj_kernel(x_ref, g_ref, w_ref, qg_ref, kg_ref, qkv0_ref, qkv1_ref, qkv2_ref, uf_ref, up_ref, gate_ref,
                    h_ref, slab_ref):
    tm = x_ref.shape[0]
    qkv_refs = (qkv0_ref, qkv1_ref, qkv2_ref)
    h_ref[...] = _rms(x_ref[...], g_ref[...]).astype(BF16)
    pieces = IN_CHUNK // LANES
    chunks = IN_COLS // IN_CHUNK

    def project(c):
        acc = jnp.dot(h_ref[...], w_ref[:, c * IN_CHUNK:(c + 1) * IN_CHUNK], preferred_element_type=F32)
        for hh in range(pieces):
            slab_ref[c % 2, hh] = acc[:, hh * LANES:(hh + 1) * LANES]

    project(0)
    for c in range(chunks):
        col = c * IN_CHUNK
        if c + 1 < chunks:
            project(c + 1)
        slabs = slab_ref.at[c % 2]
        if col < QKV_WIDTH:
            kind, group = divmod(c, len(ATTN_GROUPS))
            dilation = ATTN_GROUPS[group][1]
            out_ref = qkv_refs[group]
            for hh in range(HEADS_PER_GROUP):
                lo = kind * GROUP_WIDTH + hh * HEAD_DIM
                for r in range(dilation):
                    for r0 in range(0, tm // dilation, EPI_ROWS):
                        n_rows = min(EPI_ROWS, tm // dilation)
                        if dilation == 1:
                            a = slabs[hh, r0:r0 + n_rows, :]
                        else:
                            a = slabs[hh, pl.ds(r0 * dilation + r, n_rows, stride=dilation), :]
                        if kind < 2:
                            a = _rms(a, qg_ref[...] if kind == 0 else kg_ref[...])
                        c0 = r * GROUP_QKV + lo
                        out_ref[r0:r0 + n_rows, c0:c0 + HEAD_DIM] = a.astype(BF16)
        elif col < QKV_WIDTH + FOURIER_WIDTH + POOL_WIDTH:
            out_ref = uf_ref if col < QKV_WIDTH + FOURIER_WIDTH else up_ref
            for hh in range(pieces):
                for r0 in range(0, tm, EPI_ROWS):
                    out_ref[r0:r0 + EPI_ROWS, hh * LANES:(hh + 1) * LANES] = slabs[hh, r0:r0 + EPI_ROWS, :].astype(BF16)
        else:
            lo = col - (IN_COLS - GATE_WIDTH)
            for hh in range(pieces):
                for r0 in range(0, tm, EPI_ROWS):
                    gate = 1.0 / (1.0 + jnp.exp(-slabs[hh, r0:r0 + EPI_ROWS, :]))
                    gate_ref[r0:r0 + EPI_ROWS, lo + hh * LANES:lo + (hh + 1) * LANES] = gate.astype(BF16)


def _in_proj(x2, g, w, layer, qg, kg, tm=512):
    n = x2.shape[0]
    row = lambda i: (i, 0)
    fix = lambda i: (0, 0)
    return pl.pallas_call(
        _in_proj_kernel,
        grid=(n // tm,),
        in_specs=[pl.BlockSpec((tm, D_MODEL), row),
                  pl.BlockSpec((1, D_MODEL), fix),
                  _layer_block((D_MODEL, IN_COLS), layer, resident=True),
                  pl.BlockSpec((1, HEAD_DIM), fix),
                  pl.BlockSpec((1, HEAD_DIM), fix)],
        out_specs=[pl.BlockSpec((tm // d, d * GROUP_QKV), row) for _, d in ATTN_GROUPS]
        + [pl.BlockSpec((tm, FOURIER_WIDTH), row),
           pl.BlockSpec((tm, POOL_WIDTH), row),
           pl.BlockSpec((tm, GATE_WIDTH), row)],
        out_shape=[jax.ShapeDtypeStruct((n // d, d * GROUP_QKV), BF16) for _, d in ATTN_GROUPS]
        + [jax.ShapeDtypeStruct((n, FOURIER_WIDTH), BF16),
           jax.ShapeDtypeStruct((n, POOL_WIDTH), BF16),
           jax.ShapeDtypeStruct((n, GATE_WIDTH), BF16)],
        scratch_shapes=[pltpu.VMEM((tm, D_MODEL), BF16),
                        pltpu.VMEM((2, IN_CHUNK // LANES, tm, LANES), F32)],
        compiler_params=_cparams("parallel"),
        name="in_proj",
    )(x2, g, w, qg, kg)


def _t5_bucket(rel):
    half = N_BUCKETS // 2
    ret = (rel > 0).astype(np.int64) * half
    n = np.abs(rel)
    max_exact = half // 2
    large = max_exact + (np.log(np.maximum(n, 1) / max_exact) / np.log(REL_MAX_DISTANCE / max_exact)
                         * (half - max_exact)).astype(np.int64)
    large = np.minimum(large, half - 1)
    return (ret + np.where(n < max_exact, n, large)).astype(np.int32)


def _attn_bias(rel_bias, group):
    dilation = ATTN_GROUPS[group][1]
    r = np.arange(Q_SUB)[:, None]
    c = np.arange(K_WIN)[None, :]
    heads = rel_bias[:, group * HEADS_PER_GROUP:(group + 1) * HEADS_PER_GROUP].astype(F32)
    off = np.stack([c - variant * HALF_WINDOW - r for variant in range(3)])
    valid = np.abs(off) <= HALF_WINDOW
    bucket = _t5_bucket(np.clip(off, -HALF_WINDOW, HALF_WINDOW) * dilation)
    onehot = (jnp.asarray(bucket)[..., None] == jnp.arange(N_BUCKETS)).astype(F32)
    b = jnp.einsum("vqkn,nh->vhqk", onehot, heads, precision=lax.Precision.HIGHEST)
    return jnp.where(valid[:, None], b, NEG_INF)


def _attn_kernel(*refs, seq, tq, res, packed):
    if packed:
        qkv_ref, bias_ref, o_ref, lse_ref, vaug_ref = refs
        q_ref = k_ref = v_ref = qkv_ref
        base = lambda rr, kind: rr * GROUP_QKV + kind * GROUP_WIDTH
    else:
        q_ref, k_ref, v_ref, bias_ref, o_ref, lse_ref, vaug_ref = refs
        base = lambda rr, kind: 0
    t = pl.program_id(2)

    @pl.when(t == 0)
    def _():
        for rr in range(res):
            for h in range(HEADS_PER_GROUP):
                c = (rr * HEADS_PER_GROUP + h) * 2 * HEAD_DIM
                vc = base(rr, 2) + h * HEAD_DIM
                vaug_ref[:, c:c + HEAD_DIM] = v_ref[0, :, vc:vc + HEAD_DIM]
                vaug_ref[:, c + HEAD_DIM:c + 2 * HEAD_DIM] = jnp.ones((seq, HEAD_DIM), BF16)

    lane = lax.broadcasted_iota(jnp.int32, (Q_SUB, LANES), 1)
    for rr in range(res):
        for s in range(tq // Q_SUB):
            rows = slice(s * Q_SUB, (s + 1) * Q_SUB)
            q0 = t * tq + s * Q_SUB
            start = pl.multiple_of(jnp.clip(q0 - HALF_WINDOW, 0, seq - K_WIN), HALF_WINDOW)
            variant = (q0 - start) // HALF_WINDOW
            lse_tile = jnp.zeros((Q_SUB, LANES), F32)
            for h in range(HEADS_PER_GROUP):
                qc = base(rr, 0) + h * HEAD_DIM
                kc = base(rr, 1) + h * HEAD_DIM
                c = (rr * HEADS_PER_GROUP + h) * 2 * HEAD_DIM
                q = q_ref[0, rows, qc:qc + HEAD_DIM]
                k = k_ref[0, pl.ds(start, K_WIN), kc:kc + HEAD_DIM]
                sc = lax.dot_general(q, k, (((1,), (1,)), ((), ())), preferred_element_type=F32)
                sc = sc + bias_ref[variant, h]
                m = jnp.max(sc, axis=-1, keepdims=True)
                p = jnp.exp(sc - m)
                pv = jnp.dot(p.astype(BF16), vaug_ref[pl.ds(start, K_WIN), c:c + 2 * HEAD_DIM],
                             preferred_element_type=F32)
                den = pv[:, HEAD_DIM:]
                oc = rr * GROUP_WIDTH + h * HEAD_DIM
                o_ref[0, rows, oc:oc + HEAD_DIM] = (pv[:, :HEAD_DIM] / den).astype(BF16)
                lse_tile = jnp.where(lane == h, m + jnp.log(den), lse_tile)
            lse_ref[0, rows, rr * LANES:(rr + 1) * LANES] = lse_tile


def _attention_group(qkv, bias, group, batch, seq_full):
    dilation = ATTN_GROUPS[group][1]
    seq = seq_full // dilation
    tq = min(ATTN_ROWS, seq)
    qkv_v = qkv.reshape(batch, seq, dilation * GROUP_QKV)
    packed = tq == seq
    res = max(1, ATTN_ROWS // seq) if packed else 1
    if packed:
        qkv_specs = [pl.BlockSpec((1, seq, res * GROUP_QKV), lambda b, r, t: (b, 0, r))]
    else:
        qkv_specs = [pl.BlockSpec((1, tq, GROUP_WIDTH), lambda b, r, t: (b, t, 3 * r)),
                     pl.BlockSpec((1, seq, GROUP_WIDTH), lambda b, r, t: (b, 0, 3 * r + 1)),
                     pl.BlockSpec((1, seq, GROUP_WIDTH), lambda b, r, t: (b, 0, 3 * r + 2))]
    o, lse = pl.pallas_call(
        functools.partial(_attn_kernel, seq=seq, tq=tq, res=res, packed=packed),
        grid=(batch, dilation // res, seq // tq),
        in_specs=qkv_specs + [pl.BlockSpec((3, HEADS_PER_GROUP, Q_SUB, K_WIN), lambda b, r, t: (0, 0, 0, 0))],
        out_specs=[pl.BlockSpec((1, tq, res * GROUP_WIDTH), lambda b, r, t: (b, t, r)),
                   pl.BlockSpec((1, tq, res * LANES), lambda b, r, t: (b, t, r))],
        out_shape=[jax.ShapeDtypeStruct((batch, seq, dilation * GROUP_WIDTH), BF16),
                   jax.ShapeDtypeStruct((batch, seq, dilation * LANES), F32)],
        scratch_shapes=[pltpu.VMEM((seq, res * 2 * GROUP_WIDTH), BF16)],
        compiler_params=_cparams("parallel", "parallel", "arbitrary"),
        name=f"attn_g{group}",
    )(*([qkv_v] * len(qkv_specs)), bias)
    return o.reshape(batch * seq, dilation * GROUP_WIDTH), lse.reshape(batch * seq, dilation * LANES)


def _dft_tables(seq):
    c = jnp.arange(HEAD_DIM, dtype=jnp.int32)
    ang_c = ((c[:, None] * c[None, :]) % HEAD_DIM).astype(F32) * (2.0 * math.pi / HEAD_DIM)
    chan = jnp.concatenate([jnp.cos(ang_c), jnp.sin(ang_c)], axis=1) * (HEAD_DIM ** -0.5)
    tiles = seq // 2 // DFT_TILE
    s = jnp.arange(seq, dtype=jnp.int32)

    def angle(k):
        return ((k[:, None] * s[None, :]) % seq).astype(F32) * (2.0 * math.pi / seq)

    ang_j = angle(jnp.arange(DFT_TILE + DFT_EXTRA, dtype=jnp.int32))
    ang_m = angle(jnp.arange(tiles, dtype=jnp.int32) * DFT_TILE)
    cj, sj = jnp.cos(ang_j)[None], jnp.sin(ang_j)[None]
    cm, sm = jnp.cos(ang_m)[:, None, :], jnp.sin(ang_m)[:, None, :]
    cos_t = (cm * cj - sm * sj) * (seq ** -0.5)
    sin_t = (sm * cj + cm * sj) * (seq ** -0.5)
    j = np.arange(DFT_TILE)[:, None]
    flip = (np.arange(DFT_TILE + DFT_EXTRA)[None, :] == DFT_TILE - j).astype(np.float32)
    return chan.astype(BF16), cos_t.astype(BF16), sin_t.astype(BF16), jnp.asarray(flip, BF16)


def _fourier_kernel(u_ref, chan_ref, cos_ref, sin_ref, flip_ref, o_ref, y_ref, *, seq):
    m = pl.program_id(1)

    @pl.when(m == 0)
    def _():
        rows = 512
        for r0 in range(0, seq, rows):
            for g in range(FOURIER_GROUPS):
                y = jnp.dot(u_ref[0, r0:r0 + rows, g * HEAD_DIM:(g + 1) * HEAD_DIM], chan_ref[...],
                            preferred_element_type=F32)
                y_ref[r0:r0 + rows, g * HEAD_DIM:(g + 1) * HEAD_DIM] = y[:, :HEAD_DIM].astype(BF16)
                y_ref[seq + r0:seq + r0 + rows, g * HEAD_DIM:(g + 1) * HEAD_DIM] = y[:, HEAD_DIM:].astype(BF16)

    even = jnp.dot(cos_ref[0], y_ref[0:seq, :], preferred_element_type=F32)
    odd = jnp.dot(sin_ref[0], y_ref[seq:2 * seq, :], preferred_element_type=F32)
    lo = pl.multiple_of(m * DFT_TILE, DFT_TILE)
    o_ref[0, pl.ds(lo, DFT_TILE), :] = (even - odd)[:DFT_TILE].astype(BF16)
    mirrored = jnp.dot(flip_ref[...], (even + odd).astype(BF16), preferred_element_type=F32)
    hi = pl.multiple_of(seq - lo - DFT_TILE, DFT_TILE)
    o_ref[0, pl.ds(hi, DFT_TILE), :] = mirrored.astype(BF16)


def _fourier(uf, tables, batch, seq):
    chan, cos_t, sin_t, flip = tables
    u3 = uf.reshape(batch, seq, FOURIER_WIDTH)
    rows = DFT_TILE + DFT_EXTRA
    out = pl.pallas_call(
        functools.partial(_fourier_kernel, seq=seq),
        grid=(batch, seq // 2 // DFT_TILE),
        in_specs=[pl.BlockSpec((1, seq, FOURIER_WIDTH), lambda b, m: (b, 0, 0)),
                  pl.BlockSpec((HEAD_DIM, 2 * HEAD_DIM), lambda b, m: (0, 0)),
                  pl.BlockSpec((1, rows, seq), lambda b, m: (m, 0, 0)),
                  pl.BlockSpec((1, rows, seq), lambda b, m: (m, 0, 0)),
                  pl.BlockSpec((DFT_TILE, rows), lambda b, m: (0, 0))],
        out_specs=pl.BlockSpec((1, seq, FOURIER_WIDTH), lambda b, m: (b, 0, 0)),
        out_shape=jax.ShapeDtypeStruct((batch, seq, FOURIER_WIDTH), BF16),
        scratch_shapes=[pltpu.VMEM((2 * seq, FOURIER_WIDTH), BF16)],
        compiler_params=_cparams("parallel", "arbitrary"),
        name="fourier",
    )(u3, chan, cos_t, sin_t, flip)
    return out.reshape(batch * seq, FOURIER_WIDTH)


def _pool_kernel(u_ref, w_ref, scale_ref, o_ref, pad_ref, *, seq):
    rows = 512
    zeros = jnp.zeros((POOL_PAD, HEAD_DIM), F32)
    pad_ref[0:POOL_PAD, :] = zeros
    pad_ref[POOL_PAD + seq:2 * POOL_PAD + seq, :] = zeros
    for g, window in enumerate(POOL_WINDOWS):
        cols = slice(g * HEAD_DIM, (g + 1) * HEAD_DIM)
        half = window // 2
        pad_ref[POOL_PAD:POOL_PAD + seq, :] = u_ref[0, :, cols].astype(F32)
        for r0 in range(0, seq, rows):
            base = POOL_PAD + r0
            acc = pad_ref[base - half:base - half + rows, :]
            for off in range(-half + 1, half):
                acc = acc + pad_ref[base + off:base + off + rows, :]
            pos = r0 + lax.broadcasted_iota(jnp.int32, (rows, 1), 0)
            cnt = jnp.minimum(pos + half - 1, seq - 1) - jnp.maximum(pos - half, 0) + 1
            d = acc / cnt.astype(F32) - pad_ref[base:base + rows, :]
            y = jnp.dot(d.astype(BF16), w_ref[g], preferred_element_type=F32) * scale_ref[:, cols]
            o_ref[0, r0:r0 + rows, cols] = y.astype(BF16)


def _pool(up, w, layer, scale, batch, seq):
    u3 = up.reshape(batch, seq, POOL_WIDTH)
    out = pl.pallas_call(
        functools.partial(_pool_kernel, seq=seq),
        grid=(batch,),
        in_specs=[pl.BlockSpec((1, seq, POOL_WIDTH), lambda b: (b, 0, 0)),
                  _layer_block((len(POOL_WINDOWS), HEAD_DIM, HEAD_DIM), layer),
                  pl.BlockSpec((1, POOL_WIDTH), lambda b: (0, 0))],
        out_specs=pl.BlockSpec((1, seq, POOL_WIDTH), lambda b: (b, 0, 0)),
        out_shape=jax.ShapeDtypeStruct((batch, seq, POOL_WIDTH), BF16),
        scratch_shapes=[pltpu.VMEM((seq + 2 * POOL_PAD, HEAD_DIM), F32)],
        compiler_params=_cparams("parallel"),
        name="pool",
    )(u3, w, scale)
    return out.reshape(batch * seq, POOL_WIDTH)


def _merge_kernel(x_ref, o1_ref, o2_ref, o3_ref, l1_ref, l2_ref, l3_ref, four_ref, pool_ref, gate_ref,
                  wba_ref, wbf_ref, wbp_ref, wout_ref, gffn_ref, xo_ref, *rest):
    h_refs, (oslab_ref, lslab_ref, attn_ref, acc_ref) = rest[:-4], rest[-4:]
    tm = x_ref.shape[0]
    o_refs = (o1_ref, o2_ref, o3_ref)
    l_refs = (l1_ref, l2_ref, l3_ref)
    f = jnp.dot(four_ref[...], wbf_ref[...], preferred_element_type=F32)
    p = jnp.dot(pool_ref[...], wbp_ref[...], preferred_element_type=F32)
    acc_ref[...] = (gate_ref[:, D_MODEL:2 * D_MODEL].astype(F32) * f
                    + gate_ref[:, 2 * D_MODEL:3 * D_MODEL].astype(F32) * p)
    for g, (_, d) in enumerate(ATTN_GROUPS):
        if d == 1:
            continue
        for r in range(d):
            rows = pl.ds(r, tm // d, stride=d)
            lslab_ref[g - 1, rows, :] = l_refs[g][:, r * LANES:(r + 1) * LANES]
            for j in range(HEADS_PER_GROUP):
                lo = r * GROUP_WIDTH + j * HEAD_DIM
                oslab_ref[g - 1, j, rows, :] = o_refs[g][:, lo:lo + HEAD_DIM].astype(F32)
    for j in range(HEADS_PER_GROUP):
        cols = slice(j * HEAD_DIM, (j + 1) * HEAD_DIM)
        lses = [l1_ref[:, j:j + 1], lslab_ref[0, :, j:j + 1], lslab_ref[1, :, j:j + 1]]
        outs = [o1_ref[:, cols].astype(F32), oslab_ref[0, j], oslab_ref[1, j]]
        top = jnp.maximum(jnp.maximum(lses[0], lses[1]), lses[2])
        es = [jnp.exp(l - top) for l in lses]
        num = sum(e * o for e, o in zip(es, outs))
        attn_ref[:, cols] = (num / (es[0] + es[1] + es[2])).astype(BF16)
    a = jnp.dot(attn_ref[...], wba_ref[...], preferred_element_type=F32)
    merged = gate_ref[:, 0:D_MODEL].astype(F32) * a + acc_ref[...]
    xn = x_ref[...] + jnp.dot(merged.astype(BF16), wout_ref[...], preferred_element_type=F32)
    xo_ref[...] = xn
    if h_refs:
        h_refs[0][...] = _rms(xn, gffn_ref[...]).astype(BF16)


def _merge(x2, os_, ls_, four, pool, gates, wba, wbf, wbp, wout, layer, gffn, with_h, tm=512):
    n = x2.shape[0]
    row = lambda i: (i, 0)
    fix = lambda i: (0, 0)
    out_specs = [pl.BlockSpec((tm, D_MODEL), row)]
    out_shape = [jax.ShapeDtypeStruct((n, D_MODEL), F32)]
    if with_h:
        out_specs.append(pl.BlockSpec((tm, D_MODEL), row))
        out_shape.append(jax.ShapeDtypeStruct((n, D_MODEL), BF16))
    return pl.pallas_call(
        _merge_kernel,
        grid=(n // tm,),
        in_specs=[pl.BlockSpec((tm, D_MODEL), row)]
        + [pl.BlockSpec((tm // d, d * GROUP_WIDTH), row) for _, d in ATTN_GROUPS]
        + [pl.BlockSpec((tm // d, d * LANES), row) for _, d in ATTN_GROUPS]
        + [pl.BlockSpec((tm, FOURIER_WIDTH), row), pl.BlockSpec((tm, POOL_WIDTH), row),
           pl.BlockSpec((tm, GATE_WIDTH), row)]
        + [_layer_block((GROUP_WIDTH, D_MODEL), layer)] * 3
        + [_layer_block((D_MODEL, D_MODEL), layer), pl.BlockSpec((1, D_MODEL), fix)],
        out_specs=out_specs,
        out_shape=out_shape,
        scratch_shapes=[pltpu.VMEM((len(ATTN_GROUPS) - 1, HEADS_PER_GROUP, tm, HEAD_DIM), F32),
                        pltpu.VMEM((len(ATTN_GROUPS) - 1, tm, LANES), F32),
                        pltpu.VMEM((tm, GROUP_WIDTH), BF16),
                        pltpu.VMEM((tm, D_MODEL), F32)],
        compiler_params=_cparams("parallel"),
        name="merge",
    )(x2, *os_, *ls_, four, pool, gates, wba, wbf, wbp, wout, gffn)


FFN_CHUNK = 256


def _silu(a):
    return a * (1.0 / (1.0 + jnp.exp(-a)))


def _ffn_kernel(x_ref, h_ref, w1_ref, w3_ref, w2_ref, o_ref, g_ref):
    h = h_ref[...]
    ffn = w1_ref.shape[1]
    for c0 in range(0, ffn, FFN_CHUNK):
        a = jnp.dot(h, w1_ref[:, c0:c0 + FFN_CHUNK], preferred_element_type=F32)
        b = jnp.dot(h, w3_ref[:, c0:c0 + FFN_CHUNK], preferred_element_type=F32)
        g_ref[:, c0:c0 + FFN_CHUNK] = (_silu(a) * b).astype(BF16)
    o_ref[...] = x_ref[...] + jnp.dot(g_ref[...], w2_ref[...], preferred_element_type=F32)


def _dense_ffn(x2, h, w1, w3, w2, layer, tm=512):
    n = x2.shape[0]
    ffn = w1.shape[2]
    row = lambda i: (i, 0)
    return pl.pallas_call(
        _ffn_kernel,
        grid=(n // tm,),
        in_specs=[pl.BlockSpec((tm, D_MODEL), row), pl.BlockSpec((tm, D_MODEL), row),
                  _layer_block((D_MODEL, ffn), layer, resident=True),
                  _layer_block((D_MODEL, ffn), layer, resident=True),
                  _layer_block((ffn, D_MODEL), layer, resident=True)],
        out_specs=pl.BlockSpec((tm, D_MODEL), row),
        out_shape=jax.ShapeDtypeStruct((n, D_MODEL), F32),
        scratch_shapes=[pltpu.VMEM((tm, ffn), BF16)],
        compiler_params=_cparams("parallel"),
        name="dense_ffn",
    )(x2, h, w1, w3, w2)


R_E0, R_E1, R_G0, R_G1, R_RANK0, R_RANK1 = 0, 1, 2, 3, 4, 5


def _router_kernel(x_ref, g_ref, rw_ref, rb_ref, meta_ref, metat_ref, cnt_ref, *, tm):
    h = _rms(x_ref[...], g_ref[...])
    logits = jnp.dot(h.astype(BF16), rw_ref[...], preferred_element_type=F32) + rb_ref[...]
    lane = lax.broadcasted_iota(jnp.int32, (tm, LANES), 1)
    logits = jnp.where(lane < N_EXPERTS, logits, -jnp.inf)
    v0 = jnp.max(logits, axis=-1, keepdims=True)
    e0 = jnp.min(jnp.where(logits == v0, lane, LANES), axis=-1, keepdims=True)
    rest = jnp.where(lane == e0, -jnp.inf, logits)
    v1 = jnp.max(rest, axis=-1, keepdims=True)
    e1 = jnp.min(jnp.where(rest == v1, lane, LANES), axis=-1, keepdims=True)
    t = jnp.exp(v1 - v0)
    gate0 = 1.0 / (1.0 + t)
    gate1 = t / (1.0 + t)

    picked = jnp.logical_or(lane == e0, lane == e1)
    tri = (lax.broadcasted_iota(jnp.int32, (tm, tm), 1) < lax.broadcasted_iota(jnp.int32, (tm, tm), 0))
    before = jnp.dot(tri.astype(BF16), picked.astype(BF16), preferred_element_type=F32)
    rank0 = jnp.sum(jnp.where(lane == e0, before, 0.0), axis=-1, keepdims=True)
    rank1 = jnp.sum(jnp.where(lane == e1, before, 0.0), axis=-1, keepdims=True)

    meta = jnp.zeros((tm, LANES), F32)
    for ln, val in ((R_E0, e0.astype(F32)), (R_E1, e1.astype(F32)), (R_G0, gate0), (R_G1, gate1),
                    (R_RANK0, rank0), (R_RANK1, rank1)):
        meta = jnp.where(lane == ln, val, meta)
    meta_ref[...] = meta
    metat_ref[0] = jnp.transpose(meta)[:8]
    cnt_ref[0] = jnp.sum(picked.astype(F32), axis=0, keepdims=True)


def _router(x2, g, rw, rb, tm):
    n = x2.shape[0]
    rw_p = jnp.zeros((D_MODEL, LANES), BF16).at[:, :N_EXPERTS].set(rw.astype(BF16))
    rb_p = jnp.zeros((1, LANES), F32).at[0, :N_EXPERTS].set(rb.astype(F32))
    return pl.pallas_call(
        functools.partial(_router_kernel, tm=tm),
        grid=(n // tm,),
        in_specs=[pl.BlockSpec((tm, D_MODEL), lambda i: (i, 0)),
                  pl.BlockSpec((1, D_MODEL), lambda i: (0, 0)),
                  pl.BlockSpec((D_MODEL, LANES), lambda i: (0, 0)),
                  pl.BlockSpec((1, LANES), lambda i: (0, 0))],
        out_specs=[pl.BlockSpec((tm, LANES), lambda i: (i, 0)),
                   pl.BlockSpec((1, 8, tm), lambda i: (i, 0, 0)),
                   pl.BlockSpec((1, 1, LANES), lambda i: (i, 0, 0))],
        out_shape=[jax.ShapeDtypeStruct((n, LANES), F32),
                   jax.ShapeDtypeStruct((n // tm, 8, tm), F32),
                   jax.ShapeDtypeStruct((n // tm, 1, LANES), F32)],
        compiler_params=_cparams("parallel"),
        name="router",
    )(x2, g, rw_p, rb_p)


MOE_TILE = 512
RUN_ALIGN = 16
RUN_CHUNKS = (512, 256, 128, 64, 32, 16)
SORT_ROWS = -(-(MOE_TILE * TOP_K + N_EXPERTS * (RUN_ALIGN - 1)) // LANES) * LANES


def _run_copies(i, loc_ref, glob_ref, len_ref, make_copy, action):
    for e in range(N_EXPERTS):
        n = len_ref[i * N_EXPERTS + e]
        lo = loc_ref[i * N_EXPERTS + e]
        go = glob_ref[i * N_EXPERTS + e]
        for size in RUN_CHUNKS:
            done = n & ~(2 * size - 1)

            @pl.when((n & size) != 0)
            def _():
                action(make_copy(pl.multiple_of(lo + done, RUN_ALIGN), pl.multiple_of(go + done, RUN_ALIGN), size))


def _run_base(i, loc_ref, expert):
    base = jnp.zeros_like(expert)
    for e in range(N_EXPERTS):
        base = jnp.where(expert == e, loc_ref[i * N_EXPERTS + e], base)
    return base


def _dispatch_kernel(loc_ref, glob_ref, len_ref, x_ref, g_ref, metat_ref, xs_in_hbm, xs_hbm, sorted_ref, sem):
    del xs_in_hbm
    i = pl.program_id(0)
    h = _rms(x_ref[...], g_ref[...]).astype(BF16)
    rec = metat_ref[0]
    slot0 = _run_base(i, loc_ref, rec[R_E0:R_E0 + 1].astype(jnp.int32)) + rec[R_RANK0:R_RANK0 + 1].astype(jnp.int32)
    slot1 = _run_base(i, loc_ref, rec[R_E1:R_E1 + 1].astype(jnp.int32)) + rec[R_RANK1:R_RANK1 + 1].astype(jnp.int32)
    rows = lax.broadcasted_iota(jnp.int32, (SORT_ROWS, MOE_TILE), 0)
    perm = jnp.logical_or(rows == slot0, rows == slot1).astype(BF16)
    sorted_ref[i % 2] = jnp.dot(perm, h, preferred_element_type=F32).astype(BF16)

    def copies(tile, action):
        def make_copy(lo, go, size):
            return pltpu.make_async_copy(sorted_ref.at[tile % 2, pl.ds(lo, size), :],
                                         xs_hbm.at[pl.ds(go, size), :], sem.at[tile % 2])
        _run_copies(tile, loc_ref, glob_ref, len_ref, make_copy, action)

    copies(i, lambda c: c.start())

    @pl.when(i > 0)
    def _():
        copies(i - 1, lambda c: c.wait())

    @pl.when(i == pl.num_programs(0) - 1)
    def _():
        copies(i, lambda c: c.wait())


def _dispatch(x2, g, metat, loc, glob, plen, total):
    n = x2.shape[0]
    zeros = jnp.zeros((total, D_MODEL), BF16)
    return pl.pallas_call(
        _dispatch_kernel,
        grid_spec=pltpu.PrefetchScalarGridSpec(
            num_scalar_prefetch=3,
            grid=(n // MOE_TILE,),
            in_specs=[pl.BlockSpec((MOE_TILE, D_MODEL), lambda i, *_: (i, 0)),
                      pl.BlockSpec((1, D_MODEL), lambda i, *_: (0, 0)),
                      pl.BlockSpec((1, 8, MOE_TILE), lambda i, *_: (i, 0, 0)),
                      pl.BlockSpec(memory_space=pl.ANY)],
            out_specs=pl.BlockSpec(memory_space=pl.ANY),
            scratch_shapes=[pltpu.VMEM((2, SORT_ROWS, D_MODEL), BF16), pltpu.SemaphoreType.DMA((2,))]),
        out_shape=jax.ShapeDtypeStruct((total, D_MODEL), BF16),
        input_output_aliases={6: 0},
        compiler_params=_cparams("arbitrary"),
        name="moe_dispatch",
    )(loc, glob, plen, x2, g, metat, zeros)


MOE_CHUNK = 512


def _expert_kernel(blk_e_ref, n_used_ref, xs_ref, w1_ref, w3_ref, w2_ref, ys_ref, act_ref):
    j = pl.program_id(0)

    @pl.when(j < n_used_ref[0])
    def _():
        h = xs_ref[...]
        ffn = w1_ref.shape[2]
        for c0 in range(0, ffn, MOE_CHUNK):
            a = jnp.dot(h, w1_ref[0, :, c0:c0 + MOE_CHUNK], preferred_element_type=F32)
            b = jnp.dot(h, w3_ref[0, :, c0:c0 + MOE_CHUNK], preferred_element_type=F32)
            act_ref[:, c0:c0 + MOE_CHUNK] = (_silu(a) * b).astype(BF16)
        ys_ref[...] = jnp.dot(act_ref[...], w2_ref[0], preferred_element_type=F32).astype(BF16)

    @pl.when(j >= n_used_ref[0])
    def _():
        ys_ref[...] = jnp.zeros_like(ys_ref)


def _experts(xs, w1, w3, w2, layer, blk_e, n_used):
    total = xs.shape[0]
    ffn = w1.shape[3]
    n_blocks = total // EXPERT_BLOCK
    expert = lambda j, be, nu: (layer, be[j], 0, 0)
    return pl.pallas_call(
        _expert_kernel,
        grid_spec=pltpu.PrefetchScalarGridSpec(
            num_scalar_prefetch=2,
            grid=(n_blocks,),
            in_specs=[pl.BlockSpec((EXPERT_BLOCK, D_MODEL), lambda j, be, nu: (j, 0)),
                      _resident((None, 1, D_MODEL, ffn), expert),
                      _resident((None, 1, D_MODEL, ffn), expert),
                      _resident((None, 1, ffn, D_MODEL), expert)],
            out_specs=pl.BlockSpec((EXPERT_BLOCK, D_MODEL), lambda j, be, nu: (j, 0)),
            scratch_shapes=[pltpu.VMEM((EXPERT_BLOCK, ffn), BF16)]),
        out_shape=jax.ShapeDtypeStruct((total, D_MODEL), BF16),
        compiler_params=_cparams("arbitrary"),
        name="moe_experts",
    )(blk_e, n_used, xs, w1, w3, w2)


def _combine_kernel(loc_ref, glob_ref, len_ref, x_ref, meta_ref, ys_hbm, o_ref, buf_ref, sem):
    i = pl.program_id(0)

    def copies(tile, action):
        def make_copy(lo, go, size):
            return pltpu.make_async_copy(ys_hbm.at[pl.ds(go, size), :],
                                         buf_ref.at[tile % 2, pl.ds(lo, size), :], sem.at[tile % 2])
        _run_copies(tile, loc_ref, glob_ref, len_ref, make_copy, action)

    @pl.when(i == 0)
    def _():
        copies(i, lambda c: c.start())

    @pl.when(i + 1 < pl.num_programs(0))
    def _():
        copies(i + 1, lambda c: c.start())

    rec = meta_ref[...]
    slot0 = _run_base(i, loc_ref, rec[:, R_E0:R_E0 + 1].astype(jnp.int32)) + rec[:, R_RANK0:R_RANK0 + 1].astype(jnp.int32)
    slot1 = _run_base(i, loc_ref, rec[:, R_E1:R_E1 + 1].astype(jnp.int32)) + rec[:, R_RANK1:R_RANK1 + 1].astype(jnp.int32)
    cols = lax.broadcasted_iota(jnp.int32, (MOE_TILE, SORT_ROWS), 1)
    weight = jnp.where(cols == slot0, rec[:, R_G0:R_G0 + 1],
                       jnp.where(cols == slot1, rec[:, R_G1:R_G1 + 1], 0.0)).astype(BF16)
    copies(i, lambda c: c.wait())
    filled = loc_ref[i * N_EXPERTS + N_EXPERTS - 1] + len_ref[i * N_EXPERTS + N_EXPERTS - 1]
    rows = lax.broadcasted_iota(jnp.int32, (SORT_ROWS, 1), 0)
    picked = jnp.where(rows < filled, buf_ref[i % 2], jnp.zeros((), BF16))
    o_ref[...] = x_ref[...] + jnp.dot(weight, picked, preferred_element_type=F32)


def _combine(x2, meta, ys, loc, glob, plen):
    n = x2.shape[0]
    return pl.pallas_call(
        _combine_kernel,
        grid_spec=pltpu.PrefetchScalarGridSpec(
            num_scalar_prefetch=3,
            grid=(n // MOE_TILE,),
            in_specs=[pl.BlockSpec((MOE_TILE, D_MODEL), lambda i, *_: (i, 0)),
                      pl.BlockSpec((MOE_TILE, LANES), lambda i, *_: (i, 0)),
                      pl.BlockSpec(memory_space=pl.ANY)],
            out_specs=pl.BlockSpec((MOE_TILE, D_MODEL), lambda i, *_: (i, 0)),
            scratch_shapes=[pltpu.VMEM((2, SORT_ROWS, D_MODEL), BF16), pltpu.SemaphoreType.DMA((2,))]),
        out_shape=jax.ShapeDtypeStruct((n, D_MODEL), F32),
        compiler_params=_cparams("arbitrary"),
        name="moe_combine",
    )(loc, glob, plen, x2, meta, ys)


def _moe(x2, g, rw, rb, w1, w3, w2, layer):
    n = x2.shape[0]
    n_tiles = n // MOE_TILE
    meta, metat, cnt = _router(x2, g, rw, rb, MOE_TILE)
    cnt = cnt[:, 0, :N_EXPERTS].astype(jnp.int32)
    plen = (cnt + RUN_ALIGN - 1) // RUN_ALIGN * RUN_ALIGN
    loc = jnp.cumsum(plen, axis=1) - plen
    padded = (jnp.sum(plen, axis=0) + EXPERT_BLOCK - 1) // EXPERT_BLOCK * EXPERT_BLOCK
    pend = jnp.cumsum(padded)
    glob = (pend - padded)[None, :] + jnp.cumsum(plen, axis=0) - plen
    n_blocks = -(-(n * TOP_K + n_tiles * N_EXPERTS * (RUN_ALIGN - 1)) // EXPERT_BLOCK) + N_EXPERTS
    blk_start = jnp.arange(n_blocks, dtype=jnp.int32) * EXPERT_BLOCK
    blk_e = jnp.minimum(jnp.sum(blk_start[:, None] >= pend[None, :], axis=1), N_EXPERTS - 1).astype(jnp.int32)
    n_used = (pend[-1] // EXPERT_BLOCK).astype(jnp.int32).reshape(1)
    loc, glob, plen = (a.reshape(-1).astype(jnp.int32) for a in (loc, glob, plen))
    xs = _dispatch(x2, g, metat, loc, glob, plen, n_blocks * EXPERT_BLOCK)
    ys = _experts(xs, w1, w3, w2, layer, blk_e, n_used)
    return _combine(x2, meta, ys, loc, glob, plen)


def kernel(x, norm_mix_g, w_in, q_norm_g, k_norm_g, rel_bias, pool_w, pool_scale, w_branch_attn,
           w_branch_fourier, w_branch_pool, w_out, norm_ffn_g, ffn_w1, ffn_w3, ffn_w2,
           router_w, router_b, moe_w1, moe_w3, moe_w2):
    batch, seq, _ = x.shape
    depth = w_in.shape[0]
    n = batch * seq
    x2 = x.reshape(n, D_MODEL)
    biases = [_attn_bias(rel_bias, g) for g in range(len(ATTN_GROUPS))]
    dft = _dft_tables(seq)
    row = lambda v: v.reshape(1, -1).astype(F32)
    (w_in, pool_w, w_branch_attn, w_branch_fourier, w_branch_pool, w_out, ffn_w1, ffn_w3, ffn_w2,
     moe_w1, moe_w3, moe_w2) = (w.astype(BF16) for w in (
         w_in, pool_w, w_branch_attn, w_branch_fourier, w_branch_pool, w_out, ffn_w1, ffn_w3, ffn_w2,
         moe_w1, moe_w3, moe_w2))
    for layer in range(depth):
        *qkvs, uf, up, gates = _in_proj(x2, row(norm_mix_g[layer]), w_in, layer,
                                        row(q_norm_g[layer]) * (HEAD_DIM ** -0.5), row(k_norm_g[layer]))
        os_, ls_ = [], []
        for g in range(len(ATTN_GROUPS)):
            o, lse = _attention_group(qkvs[g], biases[g], g, batch, seq)
            os_.append(o)
            ls_.append(lse)
        four = _fourier(uf, dft, batch, seq)
        pool = _pool(up, pool_w, layer, row(pool_scale[layer]), batch, seq)
        dense = layer % 2 == 0
        j = layer // 2
        outs = _merge(x2, os_, ls_, four, pool, gates, w_branch_attn, w_branch_fourier, w_branch_pool,
                      w_out, layer, row(norm_ffn_g[layer]), with_h=dense)
        if dense:
            x2 = _dense_ffn(outs[0], outs[1], ffn_w1, ffn_w3, ffn_w2, j)
        else:
            x2 = _moe(outs[0], row(norm_ffn_g[layer]), router_w[j], router_b[j], moe_w1, moe_w3, moe_w2, j)
    return x2.reshape(batch, seq, D_MODEL)
```

```python
import functools
import math

import numpy as np
import jax
import jax.numpy as jnp
from jax import lax
from jax.experimental import pallas as pl
from jax.experimental.pallas import tpu as pltpu

F32 = jnp.float32
BF16 = jnp.bfloat16

D_MODEL = 1024
HEAD_DIM = 128
ATTN_GROUPS = ((128, 1), (512, 4), (2048, 16))
HEADS_PER_GROUP = 4
N_ATTN_HEADS = HEADS_PER_GROUP * len(ATTN_GROUPS)
ATTN_WIDTH = N_ATTN_HEADS * HEAD_DIM
GROUP_WIDTH = HEADS_PER_GROUP * HEAD_DIM
QKV_WIDTH = 3 * ATTN_WIDTH
GROUP_QKV = 3 * GROUP_WIDTH
N_BUCKETS = 32
REL_MAX_DISTANCE = 1024
FOURIER_GROUPS = 4
FOURIER_WIDTH = 512
POOL_WINDOWS = (2, 4, 8, 16)
POOL_WIDTH = 512
N_BRANCHES = 3
GATE_WIDTH = N_BRANCHES * D_MODEL
IN_COLS = QKV_WIDTH + FOURIER_WIDTH + POOL_WIDTH + GATE_WIDTH
N_EXPERTS = 8
TOP_K = 2
EXPERT_BLOCK = 512
RMS_EPS = 1e-6
NEG_INF = -1e30

LANES = 128
HALF_WINDOW = 64
Q_SUB = 128
K_WIN = Q_SUB + 2 * HALF_WINDOW
ATTN_ROWS = 1024
POOL_PAD = 8
DFT_TILE = 512
DFT_EXTRA = 8
VMEM_LIMIT = 56 * 1024 * 1024


def _cparams(*sem):
    return pltpu.CompilerParams(dimension_semantics=sem, vmem_limit_bytes=VMEM_LIMIT)


def _resident(shape, index_map):
    return pl.BlockSpec(shape, index_map, pipeline_mode=pl.Buffered(1))


def _layer_block(shape, layer, resident=False):
    index_map = lambda *_: (layer,) + (0,) * len(shape)
    make = _resident if resident else pl.BlockSpec
    return make((None,) + tuple(shape), index_map)


def _rms(x, gain):
    return x * lax.rsqrt(jnp.mean(x * x, axis=-1, keepdims=True) + RMS_EPS) * gain


IN_CHUNK = 512
EPI_ROWS = 128


def _in_proj_kernel(x_ref, g_ref, w_ref, qg_ref, kg_ref, qkv0_ref, qkv1_ref, qkv2_ref, uf_ref, up_ref, gate_ref,
                    h_ref, *slab_refs):
    tm = x_ref.shape[0]
    qkv_refs = (qkv0_ref, qkv1_ref, qkv2_ref)
    h_ref[...] = _rms(x_ref[...], g_ref[...]).astype(BF16)
    pieces = IN_CHUNK // LANES
    for c in range(IN_COLS // IN_CHUNK):
        col = c * IN_CHUNK
        acc = jnp.dot(h_ref[...], w_ref[:, col:col + IN_CHUNK], preferred_element_type=F32)
        slabs = slab_refs[(c % 2) * pieces:(c % 2 + 1) * pieces]
        for hh in range(pieces):
            slabs[hh][...] = acc[:, hh * LANES:(hh + 1) * LANES]
        if col < QKV_WIDTH:
            kind, group = divmod(c, len(ATTN_GROUPS))
            dilation = ATTN_GROUPS[group][1]
            out_ref = qkv_refs[group]
            for hh in range(HEADS_PER_GROUP):
                lo = kind * GROUP_WIDTH + hh * HEAD_DIM
                for r in range(dilation):
                    for r0 in range(0, tm // dilation, EPI_ROWS):
                        n_rows = min(EPI_ROWS, tm // dilation)
                        if dilation == 1:
                            a = slabs[hh][r0:r0 + n_rows, :]
                        else:
                            a = slabs[hh][pl.ds(r0 * dilation + r, n_rows, stride=dilation), :]
                        if kind < 2:
                            a = _rms(a, qg_ref[...] if kind == 0 else kg_ref[...])
                        c0 = r * GROUP_QKV + lo
                        out_ref[r0:r0 + n_rows, c0:c0 + HEAD_DIM] = a.astype(BF16)
        elif col < QKV_WIDTH + FOURIER_WIDTH + POOL_WIDTH:
            out_ref = uf_ref if col < QKV_WIDTH + FOURIER_WIDTH else up_ref
            for hh in range(pieces):
                for r0 in range(0, tm, EPI_ROWS):
                    out_ref[r0:r0 + EPI_ROWS, hh * LANES:(hh + 1) * LANES] = slabs[hh][r0:r0 + EPI_ROWS, :].astype(BF16)
        else:
            lo = col - (IN_COLS - GATE_WIDTH)
            for hh in range(pieces):
                for r0 in range(0, tm, EPI_ROWS):
                    gate = 0.5 * jnp.tanh(0.5 * slabs[hh][r0:r0 + EPI_ROWS, :]) + 0.5
                    gate_ref[r0:r0 + EPI_ROWS, lo + hh * LANES:lo + (hh + 1) * LANES] = gate.astype(BF16)


def _in_proj(x2, g, w, layer, qg, kg, tm=512):
    n = x2.shape[0]
    row = lambda i: (i, 0)
    fix = lambda i: (0, 0)
    return pl.pallas_call(
        _in_proj_kernel,
        grid=(n // tm,),
        in_specs=[pl.BlockSpec((tm, D_MODEL), row),
                  pl.BlockSpec((1, D_MODEL), fix),
                  _layer_block((D_MODEL, IN_COLS), layer, resident=True),
                  pl.BlockSpec((1, HEAD_DIM), fix),
                  pl.BlockSpec((1, HEAD_DIM), fix)],
        out_specs=[pl.BlockSpec((tm // d, d * GROUP_QKV), row) for _, d in ATTN_GROUPS]
        + [pl.BlockSpec((tm, FOURIER_WIDTH), row),
           pl.BlockSpec((tm, POOL_WIDTH), row),
           pl.BlockSpec((tm, GATE_WIDTH), row)],
        out_shape=[jax.ShapeDtypeStruct((n // d, d * GROUP_QKV), BF16) for _, d in ATTN_GROUPS]
        + [jax.ShapeDtypeStruct((n, FOURIER_WIDTH), BF16),
           jax.ShapeDtypeStruct((n, POOL_WIDTH), BF16),
           jax.ShapeDtypeStruct((n, GATE_WIDTH), BF16)],
        scratch_shapes=[pltpu.VMEM((tm, D_MODEL), BF16)]
        + [pltpu.VMEM((tm, LANES), F32)] * (2 * IN_CHUNK // LANES),
        compiler_params=_cparams("parallel"),
        name="in_proj",
    )(x2, g, w, qg, kg)


def _t5_bucket(rel):
    half = N_BUCKETS // 2
    ret = (rel > 0).astype(np.int64) * half
    n = np.abs(rel)
    max_exact = half // 2
    large = max_exact + (np.log(np.maximum(n, 1) / max_exact) / np.log(REL_MAX_DISTANCE / max_exact)
                         * (half - max_exact)).astype(np.int64)
    large = np.minimum(large, half - 1)
    return (ret + np.where(n < max_exact, n, large)).astype(np.int32)


def _attn_bias(rel_bias, group):
    dilation = ATTN_GROUPS[group][1]
    r = np.arange(Q_SUB)[:, None]
    c = np.arange(K_WIN)[None, :]
    heads = rel_bias[:, group * HEADS_PER_GROUP:(group + 1) * HEADS_PER_GROUP].astype(F32)
    off = np.stack([c - variant * HALF_WINDOW - r for variant in range(3)])
    valid = np.abs(off) <= HALF_WINDOW
    bucket = _t5_bucket(np.clip(off, -HALF_WINDOW, HALF_WINDOW) * dilation)
    onehot = (jnp.asarray(bucket)[..., None] == jnp.arange(N_BUCKETS)).astype(F32)
    b = jnp.einsum("vqkn,nh->vhqk", onehot, heads, precision=lax.Precision.HIGHEST)
    return jnp.where(valid[:, None], b, NEG_INF)


def _attn_kernel(*refs, seq, tq, res, packed):
    if packed:
        qkv_ref, bias_ref, o_ref, lse_ref, vaug_ref = refs
        q_ref = k_ref = v_ref = qkv_ref
        base = lambda rr, kind: rr * GROUP_QKV + kind * GROUP_WIDTH
    else:
        q_ref, k_ref, v_ref, bias_ref, o_ref, lse_ref, vaug_ref = refs
        base = lambda rr, kind: 0
    t = pl.program_id(2)

    @pl.when(t == 0)
    def _():
        for rr in range(res):
            for h in range(HEADS_PER_GROUP):
                c = (rr * HEADS_PER_GROUP + h) * 2 * HEAD_DIM
                vc = base(rr, 2) + h * HEAD_DIM
                vaug_ref[:, c:c + HEAD_DIM] = v_ref[0, :, vc:vc + HEAD_DIM]
                vaug_ref[:, c + HEAD_DIM:c + 2 * HEAD_DIM] = jnp.ones((seq, HEAD_DIM), BF16)

    lane = lax.broadcasted_iota(jnp.int32, (Q_SUB, LANES), 1)
    for rr in range(res):
        for s in range(tq // Q_SUB):
            rows = slice(s * Q_SUB, (s + 1) * Q_SUB)
            q0 = t * tq + s * Q_SUB
            start = pl.multiple_of(jnp.clip(q0 - HALF_WINDOW, 0, seq - K_WIN), HALF_WINDOW)
            variant = (q0 - start) // HALF_WINDOW
            lse_tile = jnp.zeros((Q_SUB, LANES), F32)
            for h in range(HEADS_PER_GROUP):
                qc = base(rr, 0) + h * HEAD_DIM
                kc = base(rr, 1) + h * HEAD_DIM
                c = (rr * HEADS_PER_GROUP + h) * 2 * HEAD_DIM
                q = q_ref[0, rows, qc:qc + HEAD_DIM]
                k = k_ref[0, pl.ds(start, K_WIN), kc:kc + HEAD_DIM]
                sc = lax.dot_general(q, k, (((1,), (1,)), ((), ())), preferred_element_type=F32)
                sc = sc + bias_ref[variant, h]
                m = jnp.max(sc, axis=-1, keepdims=True)
                p = jnp.exp(sc - m)
                pv = jnp.dot(p.astype(BF16), vaug_ref[pl.ds(start, K_WIN), c:c + 2 * HEAD_DIM],
                             preferred_element_type=F32)
                den = pv[:, HEAD_DIM:]
                oc = rr * GROUP_WIDTH + h * HEAD_DIM
                o_ref[0, rows, oc:oc + HEAD_DIM] = (pv[:, :HEAD_DIM] / den).astype(BF16)
                lse_tile = jnp.where(lane == h, m + jnp.log(den), lse_tile)
            lse_ref[0, rows, rr * LANES:(rr + 1) * LANES] = lse_tile


def _attention_group(qkv, bias, group, batch, seq_full):
    dilation = ATTN_GROUPS[group][1]
    seq = seq_full // dilation
    tq = min(ATTN_ROWS, seq)
    qkv_v = qkv.reshape(batch, seq, dilation * GROUP_QKV)
    packed = tq == seq
    res = max(1, ATTN_ROWS // seq) if packed else 1
    if packed:
        qkv_specs = [pl.BlockSpec((1, seq, res * GROUP_QKV), lambda b, r, t: (b, 0, r))]
    else:
        qkv_specs = [pl.BlockSpec((1, tq, GROUP_WIDTH), lambda b, r, t: (b, t, 3 * r)),
                     pl.BlockSpec((1, seq, GROUP_WIDTH), lambda b, r, t: (b, 0, 3 * r + 1)),
                     pl.BlockSpec((1, seq, GROUP_WIDTH), lambda b, r, t: (b, 0, 3 * r + 2))]
    o, lse = pl.pallas_call(
        functools.partial(_attn_kernel, seq=seq, tq=tq, res=res, packed=packed),
        grid=(batch, dilation // res, seq // tq),
        in_specs=qkv_specs + [pl.BlockSpec((3, HEADS_PER_GROUP, Q_SUB, K_WIN), lambda b, r, t: (0, 0, 0, 0))],
        out_specs=[pl.BlockSpec((1, tq, res * GROUP_WIDTH), lambda b, r, t: (b, t, r)),
                   pl.BlockSpec((1, tq, res * LANES), lambda b, r, t: (b, t, r))],
        out_shape=[jax.ShapeDtypeStruct((batch, seq, dilation * GROUP_WIDTH), BF16),
                   jax.ShapeDtypeStruct((batch, seq, dilation * LANES), F32)],
        scratch_shapes=[pltpu.VMEM((seq, res * 2 * GROUP_WIDTH), BF16)],
        compiler_params=_cparams("parallel", "parallel", "arbitrary"),
        name=f"attn_g{group}",
    )(*([qkv_v] * len(qkv_specs)), bias)
    return o.reshape(batch * seq, dilation * GROUP_WIDTH), lse.reshape(batch * seq, dilation * LANES)


def _dft_tables(seq):
    c = jnp.arange(HEAD_DIM, dtype=jnp.int32)
    ang_c = ((c[:, None] * c[None, :]) % HEAD_DIM).astype(F32) * (2.0 * math.pi / HEAD_DIM)
    chan = jnp.concatenate([jnp.cos(ang_c), jnp.sin(ang_c)], axis=1) * (HEAD_DIM ** -0.5)
    tiles = seq // 2 // DFT_TILE
    s = jnp.arange(seq, dtype=jnp.int32)

    def angle(k):
        return ((k[:, None] * s[None, :]) % seq).astype(F32) * (2.0 * math.pi / seq)

    ang_j = angle(jnp.arange(DFT_TILE + DFT_EXTRA, dtype=jnp.int32))
    ang_m = angle(jnp.arange(tiles, dtype=jnp.int32) * DFT_TILE)
    cj, sj = jnp.cos(ang_j)[None], jnp.sin(ang_j)[None]
    cm, sm = jnp.cos(ang_m)[:, None, :], jnp.sin(ang_m)[:, None, :]
    cos_t = (cm * cj - sm * sj) * (seq ** -0.5)
    sin_t = (sm * cj + cm * sj) * (seq ** -0.5)
    j = np.arange(DFT_TILE)[:, None]
    flip = (np.arange(DFT_TILE + DFT_EXTRA)[None, :] == DFT_TILE - j).astype(np.float32)
    return chan.astype(BF16), cos_t.astype(BF16), sin_t.astype(BF16), jnp.asarray(flip, BF16)


def _fourier_kernel(u_ref, chan_ref, cos_ref, sin_ref, flip_ref, o_ref, y_ref, *, seq):
    m = pl.program_id(1)

    @pl.when(m == 0)
    def _():
        rows = 512
        for r0 in range(0, seq, rows):
            for g in range(FOURIER_GROUPS):
                y = jnp.dot(u_ref[0, r0:r0 + rows, g * HEAD_DIM:(g + 1) * HEAD_DIM], chan_ref[...],
                            preferred_element_type=F32)
                y_ref[r0:r0 + rows, g * HEAD_DIM:(g + 1) * HEAD_DIM] = y[:, :HEAD_DIM].astype(BF16)
                y_ref[seq + r0:seq + r0 + rows, g * HEAD_DIM:(g + 1) * HEAD_DIM] = y[:, HEAD_DIM:].astype(BF16)

    even = jnp.dot(cos_ref[0], y_ref[0:seq, :], preferred_element_type=F32)
    odd = jnp.dot(sin_ref[0], y_ref[seq:2 * seq, :], preferred_element_type=F32)
    lo = pl.multiple_of(m * DFT_TILE, DFT_TILE)
    o_ref[0, pl.ds(lo, DFT_TILE), :] = (even - odd)[:DFT_TILE].astype(BF16)
    mirrored = jnp.dot(flip_ref[...], (even + odd).astype(BF16), preferred_element_type=F32)
    hi = pl.multiple_of(seq - lo - DFT_TILE, DFT_TILE)
    o_ref[0, pl.ds(hi, DFT_TILE), :] = mirrored.astype(BF16)


def _fourier(uf, tables, batch, seq):
    chan, cos_t, sin_t, flip = tables
    u3 = uf.reshape(batch, seq, FOURIER_WIDTH)
    rows = DFT_TILE + DFT_EXTRA
    out = pl.pallas_call(
        functools.partial(_fourier_kernel, seq=seq),
        grid=(batch, seq // 2 // DFT_TILE),
        in_specs=[pl.BlockSpec((1, seq, FOURIER_WIDTH), lambda b, m: (b, 0, 0)),
                  pl.BlockSpec((HEAD_DIM, 2 * HEAD_DIM), lambda b, m: (0, 0)),
                  pl.BlockSpec((1, rows, seq), lambda b, m: (m, 0, 0)),
                  pl.BlockSpec((1, rows, seq), lambda b, m: (m, 0, 0)),
                  pl.BlockSpec((DFT_TILE, rows), lambda b, m: (0, 0))],
        out_specs=pl.BlockSpec((1, seq, FOURIER_WIDTH), lambda b, m: (b, 0, 0)),
        out_shape=jax.ShapeDtypeStruct((batch, seq, FOURIER_WIDTH), BF16),
        scratch_shapes=[pltpu.VMEM((2 * seq, FOURIER_WIDTH), BF16)],
        compiler_params=_cparams("parallel", "arbitrary"),
        name="fourier",
    )(u3, chan, cos_t, sin_t, flip)
    return out.reshape(batch * seq, FOURIER_WIDTH)


def _pool_kernel(u_ref, w_ref, scale_ref, o_ref, pad_ref, *, seq):
    rows = 512
    zeros = jnp.zeros((POOL_PAD, HEAD_DIM), F32)
    pad_ref[0:POOL_PAD, :] = zeros
    pad_ref[POOL_PAD + seq:2 * POOL_PAD + seq, :] = zeros
    for g, window in enumerate(POOL_WINDOWS):
        cols = slice(g * HEAD_DIM, (g + 1) * HEAD_DIM)
        half = window // 2
        pad_ref[POOL_PAD:POOL_PAD + seq, :] = u_ref[0, :, cols].astype(F32)
        for r0 in range(0, seq, rows):
            base = POOL_PAD + r0
            acc = pad_ref[base - half:base - half + rows, :]
            for off in range(-half + 1, half):
                acc = acc + pad_ref[base + off:base + off + rows, :]
            pos = r0 + lax.broadcasted_iota(jnp.int32, (rows, 1), 0)
            cnt = jnp.minimum(pos + half - 1, seq - 1) - jnp.maximum(pos - half, 0) + 1
            d = acc / cnt.astype(F32) - pad_ref[base:base + rows, :]
            y = jnp.dot(d.astype(BF16), w_ref[g], preferred_element_type=F32) * scale_ref[:, cols]
            o_ref[0, r0:r0 + rows, cols] = y.astype(BF16)


def _pool(up, w, layer, scale, batch, seq):
    u3 = up.reshape(batch, seq, POOL_WIDTH)
    out = pl.pallas_call(
        functools.partial(_pool_kernel, seq=seq),
        grid=(batch,),
        in_specs=[pl.BlockSpec((1, seq, POOL_WIDTH), lambda b: (b, 0, 0)),
                  _layer_block((len(POOL_WINDOWS), HEAD_DIM, HEAD_DIM), layer),
                  pl.BlockSpec((1, POOL_WIDTH), lambda b: (0, 0))],
        out_specs=pl.BlockSpec((1, seq, POOL_WIDTH), lambda b: (b, 0, 0)),
        out_shape=jax.ShapeDtypeStruct((batch, seq, POOL_WIDTH), BF16),
        scratch_shapes=[pltpu.VMEM((seq + 2 * POOL_PAD, HEAD_DIM), F32)],
        compiler_params=_cparams("parallel"),
        name="pool",
    )(u3, w, scale)
    return out.reshape(batch * seq, POOL_WIDTH)


def _merge_kernel(x_ref, o1_ref, o2_ref, o3_ref, l1_ref, l2_ref, l3_ref, four_ref, pool_ref, gate_ref,
                  wba_ref, wbf_ref, wbp_ref, wout_ref, gffn_ref, xo_ref, *rest):
    n_dilated = len(ATTN_GROUPS) - 1
    n_scratch = n_dilated * HEADS_PER_GROUP + n_dilated + 2
    h_refs, scratch = rest[:-n_scratch], rest[-n_scratch:]
    oslabs = scratch[:n_dilated * HEADS_PER_GROUP]
    lslabs = scratch[n_dilated * HEADS_PER_GROUP:-2]
    attn_ref, acc_ref = scratch[-2:]
    tm = x_ref.shape[0]
    o_refs = (o1_ref, o2_ref, o3_ref)
    l_refs = (l1_ref, l2_ref, l3_ref)
    f = jnp.dot(four_ref[...], wbf_ref[...], preferred_element_type=F32)
    p = jnp.dot(pool_ref[...], wbp_ref[...], preferred_element_type=F32)
    acc_ref[...] = (gate_ref[:, D_MODEL:2 * D_MODEL].astype(F32) * f
                    + gate_ref[:, 2 * D_MODEL:3 * D_MODEL].astype(F32) * p)
    dilated = [(g, d) for g, (_, d) in enumerate(ATTN_GROUPS) if d > 1]
    for g, d in dilated:
        for r in range(d):
            lslabs[g - 1][pl.ds(r, tm // d, stride=d), :] = l_refs[g][:, r * LANES:(r + 1) * LANES]
    lses = [l1_ref[...]] + [ref[...] for ref in lslabs]
    top = functools.reduce(jnp.maximum, lses)
    es = [jnp.exp(l - top) for l in lses]
    inv = 1.0 / functools.reduce(lambda a, b: a + b, es)
    weights = [e * inv for e in es]
    for j in range(HEADS_PER_GROUP):
        cols = slice(j * HEAD_DIM, (j + 1) * HEAD_DIM)
        for g, d in dilated:
            slab = oslabs[(g - 1) * HEADS_PER_GROUP + j]
            for r in range(d):
                lo = r * GROUP_WIDTH + j * HEAD_DIM
                slab[pl.ds(r, tm // d, stride=d), :] = o_refs[g][:, lo:lo + HEAD_DIM].astype(F32)
        outs = [o1_ref[:, cols].astype(F32)] + [oslabs[(g - 1) * HEADS_PER_GROUP + j][...] for g, _ in dilated]
        attn_ref[:, cols] = sum(w[:, j:j + 1] * o for w, o in zip(weights, outs)).astype(BF16)
    a = jnp.dot(attn_ref[...], wba_ref[...], preferred_element_type=F32)
    merged = gate_ref[:, 0:D_MODEL].astype(F32) * a + acc_ref[...]
    xn = x_ref[...] + jnp.dot(merged.astype(BF16), wout_ref[...], preferred_element_type=F32)
    xo_ref[...] = xn
    if h_refs:
        h_refs[0][...] = _rms(xn, gffn_ref[...]).astype(BF16)


def _merge(x2, os_, ls_, four, pool, gates, wba, wbf, wbp, wout, layer, gffn, with_h, tm=512):
    n = x2.shape[0]
    row = lambda i: (i, 0)
    fix = lambda i: (0, 0)
    out_specs = [pl.BlockSpec((tm, D_MODEL), row)]
    out_shape = [jax.ShapeDtypeStruct((n, D_MODEL), F32)]
    if with_h:
        out_specs.append(pl.BlockSpec((tm, D_MODEL), row))
        out_shape.append(jax.ShapeDtypeStruct((n, D_MODEL), BF16))
    return pl.pallas_call(
        _merge_kernel,
        grid=(n // tm,),
        in_specs=[pl.BlockSpec((tm, D_MODEL), row)]
        + [pl.BlockSpec((tm // d, d * GROUP_WIDTH), row) for _, d in ATTN_GROUPS]
        + [pl.BlockSpec((tm // d, d * LANES), row) for _, d in ATTN_GROUPS]
        + [pl.BlockSpec((tm, FOURIER_WIDTH), row), pl.BlockSpec((tm, POOL_WIDTH), row),
           pl.BlockSpec((tm, GATE_WIDTH), row)]
        + [_layer_block((GROUP_WIDTH, D_MODEL), layer)] * 3
        + [_layer_block((D_MODEL, D_MODEL), layer), pl.BlockSpec((1, D_MODEL), fix)],
        out_specs=out_specs,
        out_shape=out_shape,
        scratch_shapes=[pltpu.VMEM((tm, LANES), F32)] * ((len(ATTN_GROUPS) - 1) * (HEADS_PER_GROUP + 1))
        + [pltpu.VMEM((tm, GROUP_WIDTH), BF16), pltpu.VMEM((tm, D_MODEL), F32)],
        compiler_params=_cparams("parallel"),
        name="merge",
    )(x2, *os_, *ls_, four, pool, gates, wba, wbf, wbp, wout, gffn)


FFN_CHUNK = 256


def _silu(a):
    return a * (0.5 * jnp.tanh(0.5 * a) + 0.5)


def _ffn_kernel(x_ref, h_ref, w1_ref, w3_ref, w2_ref, o_ref, g_ref):
    h = h_ref[...]
    ffn = w1_ref.shape[1]
    for c0 in range(0, ffn, FFN_CHUNK):
        a = jnp.dot(h, w1_ref[:, c0:c0 + FFN_CHUNK], preferred_element_type=F32)
        b = jnp.dot(h, w3_ref[:, c0:c0 + FFN_CHUNK], preferred_element_type=F32)
        g_ref[:, c0:c0 + FFN_CHUNK] = (_silu(a) * b).astype(BF16)
    o_ref[...] = x_ref[...] + jnp.dot(g_ref[...], w2_ref[...], preferred_element_type=F32)


def _dense_ffn(x2, h, w1, w3, w2, layer, tm=512):
    n = x2.shape[0]
    ffn = w1.shape[2]
    row = lambda i: (i, 0)
    return pl.pallas_call(
        _ffn_kernel,
        grid=(n // tm,),
        in_specs=[pl.BlockSpec((tm, D_MODEL), row), pl.BlockSpec((tm, D_MODEL), row),
                  _layer_block((D_MODEL, ffn), layer, resident=True),
                  _layer_block((D_MODEL, ffn), layer, resident=True),
                  _layer_block((ffn, D_MODEL), layer, resident=True)],
        out_specs=pl.BlockSpec((tm, D_MODEL), row),
        out_shape=jax.ShapeDtypeStruct((n, D_MODEL), F32),
        scratch_shapes=[pltpu.VMEM((tm, ffn), BF16)],
        compiler_params=_cparams("parallel"),
        name="dense_ffn",
    )(x2, h, w1, w3, w2)


R_E0, R_E1, R_G0, R_G1, R_RANK0, R_RANK1 = 0, 1, 2, 3, 4, 5


def _router_kernel(x_ref, g_ref, rw_ref, rb_ref, meta_ref, metat_ref, cnt_ref, *, tm):
    h = _rms(x_ref[...], g_ref[...])
    logits = jnp.dot(h.astype(BF16), rw_ref[...], preferred_element_type=F32) + rb_ref[...]
    lane = lax.broadcasted_iota(jnp.int32, (tm, LANES), 1)
    logits = jnp.where(lane < N_EXPERTS, logits, -jnp.inf)
    v0 = jnp.max(logits, axis=-1, keepdims=True)
    e0 = jnp.min(jnp.where(logits == v0, lane, LANES), axis=-1, keepdims=True)
    rest = jnp.where(lane == e0, -jnp.inf, logits)
    v1 = jnp.max(rest, axis=-1, keepdims=True)
    e1 = jnp.min(jnp.where(rest == v1, lane, LANES), axis=-1, keepdims=True)
    t = jnp.exp(v1 - v0)
    gate0 = 1.0 / (1.0 + t)
    gate1 = t / (1.0 + t)

    picked = jnp.logical_or(lane == e0, lane == e1)
    tri = (lax.broadcasted_iota(jnp.int32, (tm, tm), 1) < lax.broadcasted_iota(jnp.int32, (tm, tm), 0))
    before = jnp.dot(tri.astype(BF16), picked.astype(BF16), preferred_element_type=F32)
    rank0 = jnp.sum(jnp.where(lane == e0, before, 0.0), axis=-1, keepdims=True)
    rank1 = jnp.sum(jnp.where(lane == e1, before, 0.0), axis=-1, keepdims=True)

    meta = jnp.zeros((tm, LANES), F32)
    for ln, val in ((R_E0, e0.astype(F32)), (R_E1, e1.astype(F32)), (R_G0, gate0), (R_G1, gate1),
                    (R_RANK0, rank0), (R_RANK1, rank1)):
        meta = jnp.where(lane == ln, val, meta)
    meta_ref[...] = meta
    metat_ref[0] = jnp.transpose(meta)[:8]
    cnt_ref[0] = jnp.sum(picked.astype(F32), axis=0, keepdims=True)


def _router(x2, g, rw, rb, tm):
    n = x2.shape[0]
    rw_p = jnp.zeros((D_MODEL, LANES), BF16).at[:, :N_EXPERTS].set(rw.astype(BF16))
    rb_p = jnp.zeros((1, LANES), F32).at[0, :N_EXPERTS].set(rb.astype(F32))
    return pl.pallas_call(
        functools.partial(_router_kernel, tm=tm),
        grid=(n // tm,),
        in_specs=[pl.BlockSpec((tm, D_MODEL), lambda i: (i, 0)),
                  pl.BlockSpec((1, D_MODEL), lambda i: (0, 0)),
                  pl.BlockSpec((D_MODEL, LANES), lambda i: (0, 0)),
                  pl.BlockSpec((1, LANES), lambda i: (0, 0))],
        out_specs=[pl.BlockSpec((tm, LANES), lambda i: (i, 0)),
                   pl.BlockSpec((1, 8, tm), lambda i: (i, 0, 0)),
                   pl.BlockSpec((1, 1, LANES), lambda i: (i, 0, 0))],
        out_shape=[jax.ShapeDtypeStruct((n, LANES), F32),
                   jax.ShapeDtypeStruct((n // tm, 8, tm), F32),
                   jax.ShapeDtypeStruct((n // tm, 1, LANES), F32)],
        compiler_params=_cparams("parallel"),
        name="router",
    )(x2, g, rw_p, rb_p)


MOE_TILE = 512
RUN_ALIGN = 16
RUN_CHUNKS = (512, 256, 128, 64, 32, 16)
SORT_ROWS = -(-(MOE_TILE * TOP_K + N_EXPERTS * (RUN_ALIGN - 1)) // LANES) * LANES


def _run_copies(i, loc_ref, glob_ref, len_ref, make_copy, action):
    for e in range(N_EXPERTS):
        n = len_ref[i * N_EXPERTS + e]
        lo = loc_ref[i * N_EXPERTS + e]
        go = glob_ref[i * N_EXPERTS + e]
        for size in RUN_CHUNKS:
            done = n & ~(2 * size - 1)

            @pl.when((n & size) != 0)
            def _():
                action(make_copy(pl.multiple_of(lo + done, RUN_ALIGN), pl.multiple_of(go + done, RUN_ALIGN), size))


def _run_base(i, loc_ref, expert):
    base = jnp.zeros_like(expert)
    for e in range(N_EXPERTS):
        base = jnp.where(expert == e, loc_ref[i * N_EXPERTS + e], base)
    return base


def _dispatch_kernel(loc_ref, glob_ref, len_ref, x_ref, g_ref, metat_ref, xs_hbm, sorted_ref, zero_ref, sem):
    i = pl.program_id(0)
    n_tiles = pl.num_programs(0)

    @pl.when(i == 0)
    def _():
        zero_ref[...] = jnp.zeros_like(zero_ref)

        def make_copy(lo, go, size):
            return pltpu.make_async_copy(zero_ref.at[pl.ds(lo, size), :], xs_hbm.at[pl.ds(go, size), :], sem.at[2])

        _run_copies(n_tiles, loc_ref, glob_ref, len_ref, make_copy, lambda c: c.start())
        _run_copies(n_tiles, loc_ref, glob_ref, len_ref, make_copy, lambda c: c.wait())
        last = n_tiles * N_EXPERTS + N_EXPERTS - 1
        first_unused = (glob_ref[last] + len_ref[last]) // EXPERT_BLOCK

        def clear(j, carry):
            block = make_copy(0, pl.multiple_of(j * EXPERT_BLOCK, EXPERT_BLOCK), EXPERT_BLOCK)
            block.start()
            block.wait()
            return carry

        lax.fori_loop(first_unused, xs_hbm.shape[0] // EXPERT_BLOCK, clear, 0)

    h = _rms(x_ref[...], g_ref[...]).astype(BF16)
    rec = metat_ref[0]
    slot0 = _run_base(i, loc_ref, rec[R_E0:R_E0 + 1].astype(jnp.int32)) + rec[R_RANK0:R_RANK0 + 1].astype(jnp.int32)
    slot1 = _run_base(i, loc_ref, rec[R_E1:R_E1 + 1].astype(jnp.int32)) + rec[R_RANK1:R_RANK1 + 1].astype(jnp.int32)
    rows = lax.broadcasted_iota(jnp.int32, (SORT_ROWS, MOE_TILE), 0)
    perm = jnp.logical_or(rows == slot0, rows == slot1).astype(BF16)
    sorted_ref[i % 2] = jnp.dot(perm, h, preferred_element_type=F32).astype(BF16)

    def copies(tile, action):
        def make_copy(lo, go, size):
            return pltpu.make_async_copy(sorted_ref.at[tile % 2, pl.ds(lo, size), :],
                                         xs_hbm.at[pl.ds(go, size), :], sem.at[tile % 2])
        _run_copies(tile, loc_ref, glob_ref, len_ref, make_copy, action)

    copies(i, lambda c: c.start())

    @pl.when(i > 0)
    def _():
        copies(i - 1, lambda c: c.wait())

    @pl.when(i == n_tiles - 1)
    def _():
        copies(i, lambda c: c.wait())


def _dispatch(x2, g, metat, loc, glob, plen, total):
    n = x2.shape[0]
    return pl.pallas_call(
        _dispatch_kernel,
        grid_spec=pltpu.PrefetchScalarGridSpec(
            num_scalar_prefetch=3,
            grid=(n // MOE_TILE,),
            in_specs=[pl.BlockSpec((MOE_TILE, D_MODEL), lambda i, *_: (i, 0)),
                      pl.BlockSpec((1, D_MODEL), lambda i, *_: (0, 0)),
                      pl.BlockSpec((1, 8, MOE_TILE), lambda i, *_: (i, 0, 0))],
            out_specs=pl.BlockSpec(memory_space=pl.ANY),
            scratch_shapes=[pltpu.VMEM((2, SORT_ROWS, D_MODEL), BF16), pltpu.VMEM((EXPERT_BLOCK, D_MODEL), BF16),
                            pltpu.SemaphoreType.DMA((3,))]),
        out_shape=jax.ShapeDtypeStruct((total, D_MODEL), BF16),
        compiler_params=_cparams("arbitrary"),
        name="moe_dispatch",
    )(loc, glob, plen, x2, g, metat)


MOE_CHUNK = 512


def _expert_kernel(blk_e_ref, n_used_ref, xs_ref, w1_ref, w3_ref, w2_ref, ys_ref, act_ref):
    j = pl.program_id(0)

    @pl.when(j < n_used_ref[0])
    def _():
        h = xs_ref[...]
        ffn = w1_ref.shape[2]
        for c0 in range(0, ffn, MOE_CHUNK):
            a = jnp.dot(h, w1_ref[0, :, c0:c0 + MOE_CHUNK], preferred_element_type=F32)
            b = jnp.dot(h, w3_ref[0, :, c0:c0 + MOE_CHUNK], preferred_element_type=F32)
            act_ref[:, c0:c0 + MOE_CHUNK] = (_silu(a) * b).astype(BF16)
        ys_ref[...] = jnp.dot(act_ref[...], w2_ref[0], preferred_element_type=F32).astype(BF16)

    @pl.when(j >= n_used_ref[0])
    def _():
        ys_ref[...] = jnp.zeros_like(ys_ref)


def _experts(xs, w1, w3, w2, layer, blk_e, n_used):
    total = xs.shape[0]
    ffn = w1.shape[3]
    n_blocks = total // EXPERT_BLOCK
    expert = lambda j, be, nu: (layer, be[j], 0, 0)
    return pl.pallas_call(
        _expert_kernel,
        grid_spec=pltpu.PrefetchScalarGridSpec(
            num_scalar_prefetch=2,
            grid=(n_blocks,),
            in_specs=[pl.BlockSpec((EXPERT_BLOCK, D_MODEL), lambda j, be, nu: (jnp.minimum(j, nu[0] - 1), 0)),
                      _resident((None, 1, D_MODEL, ffn), expert),
                      _resident((None, 1, D_MODEL, ffn), expert),
                      _resident((None, 1, ffn, D_MODEL), expert)],
            out_specs=pl.BlockSpec((EXPERT_BLOCK, D_MODEL), lambda j, be, nu: (j, 0)),
            scratch_shapes=[pltpu.VMEM((EXPERT_BLOCK, ffn), BF16)]),
        out_shape=jax.ShapeDtypeStruct((total, D_MODEL), BF16),
        compiler_params=_cparams("arbitrary"),
        name="moe_experts",
    )(blk_e, n_used, xs, w1, w3, w2)


def _combine_kernel(loc_ref, glob_ref, len_ref, x_ref, meta_ref, ys_hbm, o_ref, buf_ref, sem):
    i = pl.program_id(0)

    def copies(tile, action):
        def make_copy(lo, go, size):
            return pltpu.make_async_copy(ys_hbm.at[pl.ds(go, size), :],
                                         buf_ref.at[tile % 2, pl.ds(lo, size), :], sem.at[tile % 2])
        _run_copies(tile, loc_ref, glob_ref, len_ref, make_copy, action)

    @pl.when(i == 0)
    def _():
        copies(i, lambda c: c.start())

    @pl.when(i + 1 < pl.num_programs(0))
    def _():
        copies(i + 1, lambda c: c.start())

    rec = meta_ref[...]
    slot0 = _run_base(i, loc_ref, rec[:, R_E0:R_E0 + 1].astype(jnp.int32)) + rec[:, R_RANK0:R_RANK0 + 1].astype(jnp.int32)
    slot1 = _run_base(i, loc_ref, rec[:, R_E1:R_E1 + 1].astype(jnp.int32)) + rec[:, R_RANK1:R_RANK1 + 1].astype(jnp.int32)
    cols = lax.broadcasted_iota(jnp.int32, (MOE_TILE, SORT_ROWS), 1)
    weight = jnp.where(cols == slot0, rec[:, R_G0:R_G0 + 1],
                       jnp.where(cols == slot1, rec[:, R_G1:R_G1 + 1], 0.0)).astype(BF16)
    copies(i, lambda c: c.wait())
    filled = loc_ref[i * N_EXPERTS + N_EXPERTS - 1] + len_ref[i * N_EXPERTS + N_EXPERTS - 1]
    rows = lax.broadcasted_iota(jnp.int32, (SORT_ROWS, 1), 0)
    picked = jnp.where(rows < filled, buf_ref[i % 2], jnp.zeros((), BF16))
    o_ref[...] = x_ref[...] + jnp.dot(weight, picked, preferred_element_type=F32)


def _combine(x2, meta, ys, loc, glob, plen):
    n = x2.shape[0]
    return pl.pallas_call(
        _combine_kernel,
        grid_spec=pltpu.PrefetchScalarGridSpec(
            num_scalar_prefetch=3,
            grid=(n // MOE_TILE,),
            in_specs=[pl.BlockSpec((MOE_TILE, D_MODEL), lambda i, *_: (i, 0)),
                      pl.BlockSpec((MOE_TILE, LANES), lambda i, *_: (i, 0)),
                      pl.BlockSpec(memory_space=pl.ANY)],
            out_specs=pl.BlockSpec((MOE_TILE, D_MODEL), lambda i, *_: (i, 0)),
            scratch_shapes=[pltpu.VMEM((2, SORT_ROWS, D_MODEL), BF16), pltpu.SemaphoreType.DMA((2,))]),
        out_shape=jax.ShapeDtypeStruct((n, D_MODEL), F32),
        compiler_params=_cparams("arbitrary"),
        name="moe_combine",
    )(loc, glob, plen, x2, meta, ys)


def _moe(x2, g, rw, rb, w1, w3, w2, layer):
    n = x2.shape[0]
    n_tiles = n // MOE_TILE
    meta, metat, cnt = _router(x2, g, rw, rb, MOE_TILE)
    cnt = cnt[:, 0, :N_EXPERTS].astype(jnp.int32)
    plen = (cnt + RUN_ALIGN - 1) // RUN_ALIGN * RUN_ALIGN
    loc = jnp.cumsum(plen, axis=1) - plen
    filled = jnp.sum(plen, axis=0)
    padded = (filled + EXPERT_BLOCK - 1) // EXPERT_BLOCK * EXPERT_BLOCK
    pend = jnp.cumsum(padded)
    glob = (pend - padded)[None, :] + jnp.cumsum(plen, axis=0) - plen
    loc = jnp.concatenate([loc, jnp.zeros((1, N_EXPERTS), jnp.int32)])
    glob = jnp.concatenate([glob, (pend - padded + filled)[None, :]])
    plen = jnp.concatenate([plen, (padded - filled)[None, :]])
    n_blocks = -(-(n * TOP_K + n_tiles * N_EXPERTS * (RUN_ALIGN - 1)) // EXPERT_BLOCK) + N_EXPERTS
    blk_start = jnp.arange(n_blocks, dtype=jnp.int32) * EXPERT_BLOCK
    blk_e = jnp.minimum(jnp.sum(blk_start[:, None] >= pend[None, :], axis=1), N_EXPERTS - 1).astype(jnp.int32)
    n_used = (pend[-1] // EXPERT_BLOCK).astype(jnp.int32).reshape(1)
    loc, glob, plen = (a.reshape(-1).astype(jnp.int32) for a in (loc, glob, plen))
    xs = _dispatch(x2, g, metat, loc, glob, plen, n_blocks * EXPERT_BLOCK)
    ys = _experts(xs, w1, w3, w2, layer, blk_e, n_used)
    return _combine(x2, meta, ys, loc, glob, plen)


def kernel(x, norm_mix_g, w_in, q_norm_g, k_norm_g, rel_bias, pool_w, pool_scale, w_branch_attn,
           w_branch_fourier, w_branch_pool, w_out, norm_ffn_g, ffn_w1, ffn_w3, ffn_w2,
           router_w, router_b, moe_w1, moe_w3, moe_w2):
    batch, seq, _ = x.shape
    depth = w_in.shape[0]
    n = batch * seq
    x2 = x.reshape(n, D_MODEL)
    biases = [_attn_bias(rel_bias, g) for g in range(len(ATTN_GROUPS))]
    dft = _dft_tables(seq)
    row = lambda v: v.reshape(1, -1).astype(F32)
    (w_in, pool_w, w_branch_attn, w_branch_fourier, w_branch_pool, w_out, ffn_w1, ffn_w3, ffn_w2,
     moe_w1, moe_w3, moe_w2) = (w.astype(BF16) for w in (
         w_in, pool_w, w_branch_attn, w_branch_fourier, w_branch_pool, w_out, ffn_w1, ffn_w3, ffn_w2,
         moe_w1, moe_w3, moe_w2))
    for layer in range(depth):
        *qkvs, uf, up, gates = _in_proj(x2, row(norm_mix_g[layer]), w_in, layer,
                                        row(q_norm_g[layer]) * (HEAD_DIM ** -0.5), row(k_norm_g[layer]))
        os_, ls_ = [], []
        for g in range(len(ATTN_GROUPS)):
            o, lse = _attention_group(qkvs[g], biases[g], g, batch, seq)
            os_.append(o)
            ls_.append(lse)
        four = _fourier(uf, dft, batch, seq)
        pool = _pool(up, pool_w, layer, row(pool_scale[layer]), batch, seq)
        dense = layer % 2 == 0
        j = layer // 2
        outs = _merge(x2, os_, ls_, four, pool, gates, w_branch_attn, w_branch_fourier, w_branch_pool,
                      w_out, layer, row(norm_ffn_g[layer]), with_h=dense)
        if dense:
            x2 = _dense_ffn(outs[0], outs[1], ffn_w1, ffn_w3, ffn_w2, j)
        else:
            x2 = _moe(outs[0], row(norm_ffn_g[layer]), router_w[j], router_b[j], moe_w1, moe_w3, moe_w2, j)
    return x2.reshape(batch, seq, D_MODEL)
```

```python
import functools
import math

import numpy as np
import jax
import jax.numpy as jnp
from jax import lax
from jax.experimental import pallas as pl
from jax.experimental.pallas import tpu as pltpu

F32 = jnp.float32
BF16 = jnp.bfloat16

D_MODEL = 1024
HEAD_DIM = 128
ATTN_GROUPS = ((128, 1), (512, 4), (2048, 16))
HEADS_PER_GROUP = 4
N_ATTN_HEADS = HEADS_PER_GROUP * len(ATTN_GROUPS)
ATTN_WIDTH = N_ATTN_HEADS * HEAD_DIM
GROUP_WIDTH = HEADS_PER_GROUP * HEAD_DIM
QKV_WIDTH = 3 * ATTN_WIDTH
GROUP_QKV = 3 * GROUP_WIDTH
N_BUCKETS = 32
REL_MAX_DISTANCE = 1024
FOURIER_GROUPS = 4
FOURIER_WIDTH = 512
POOL_WINDOWS = (2, 4, 8, 16)
POOL_WIDTH = 512
N_BRANCHES = 3
GATE_WIDTH = N_BRANCHES * D_MODEL
IN_COLS = QKV_WIDTH + FOURIER_WIDTH + POOL_WIDTH + GATE_WIDTH
N_EXPERTS = 8
TOP_K = 2
EXPERT_BLOCK = 512
RMS_EPS = 1e-6
NEG_INF = -1e30

LANES = 128
HALF_WINDOW = 64
Q_SUB = 128
K_WIN = Q_SUB + 2 * HALF_WINDOW
POOL_PAD = 8
DFT_TILE = 512
DFT_EXTRA = 8
VMEM_LIMIT = 56 * 1024 * 1024


def _cparams(*sem):
    return pltpu.CompilerParams(dimension_semantics=sem, vmem_limit_bytes=VMEM_LIMIT)


def _resident(shape, index_map):
    return pl.BlockSpec(shape, index_map, pipeline_mode=pl.Buffered(1))


def _layer_block(shape, layer, resident=False):
    index_map = lambda *_: (layer,) + (0,) * len(shape)
    make = _resident if resident else pl.BlockSpec
    return make((None,) + tuple(shape), index_map)


def _rms(x, gain):
    return x * lax.rsqrt(jnp.mean(x * x, axis=-1, keepdims=True) + RMS_EPS) * gain


IN_CHUNK = 512
EPI_ROWS = 128


def _in_proj_kernel(x_ref, g_ref, w_ref, qg_ref, kg_ref, qkv0_ref, qkv1_ref, qkv2_ref, uf_ref, up_ref, gate_ref,
                    h_ref, *slab_refs):
    tm = x_ref.shape[0]
    qkv_refs = (qkv0_ref, qkv1_ref, qkv2_ref)
    h_ref[...] = _rms(x_ref[...], g_ref[...]).astype(BF16)
    pieces = IN_CHUNK // LANES
    for c in range(IN_COLS // IN_CHUNK):
        col = c * IN_CHUNK
        acc = jnp.dot(h_ref[...], w_ref[:, col:col + IN_CHUNK], preferred_element_type=F32)
        slabs = slab_refs[(c % 2) * pieces:(c % 2 + 1) * pieces]
        for hh in range(pieces):
            slabs[hh][...] = acc[:, hh * LANES:(hh + 1) * LANES]
        if col < QKV_WIDTH:
            kind, group = divmod(c, len(ATTN_GROUPS))
            dilation = ATTN_GROUPS[group][1]
            out_ref = qkv_refs[group]
            for hh in range(HEADS_PER_GROUP):
                lo = kind * GROUP_WIDTH + hh * HEAD_DIM
                for r in range(dilation):
                    for r0 in range(0, tm // dilation, EPI_ROWS):
                        n_rows = min(EPI_ROWS, tm // dilation)
                        if dilation == 1:
                            a = slabs[hh][r0:r0 + n_rows, :]
                        else:
                            a = slabs[hh][pl.ds(r0 * dilation + r, n_rows, stride=dilation), :]
                        if kind < 2:
                            a = _rms(a, qg_ref[...] if kind == 0 else kg_ref[...])
                        c0 = r * GROUP_QKV + lo
                        out_ref[r0:r0 + n_rows, c0:c0 + HEAD_DIM] = a.astype(BF16)
        elif col < QKV_WIDTH + FOURIER_WIDTH + POOL_WIDTH:
            out_ref = uf_ref if col < QKV_WIDTH + FOURIER_WIDTH else up_ref
            for hh in range(pieces):
                for r0 in range(0, tm, EPI_ROWS):
                    out_ref[r0:r0 + EPI_ROWS, hh * LANES:(hh + 1) * LANES] = slabs[hh][r0:r0 + EPI_ROWS, :].astype(BF16)
        else:
            lo = col - (IN_COLS - GATE_WIDTH)
            for hh in range(pieces):
                for r0 in range(0, tm, EPI_ROWS):
                    gate = 0.5 * jnp.tanh(0.5 * slabs[hh][r0:r0 + EPI_ROWS, :]) + 0.5
                    gate_ref[r0:r0 + EPI_ROWS, lo + hh * LANES:lo + (hh + 1) * LANES] = gate.astype(BF16)


def _in_proj(x2, g, w, layer, qg, kg, tm=512):
    n = x2.shape[0]
    row = lambda i: (i, 0)
    fix = lambda i: (0, 0)
    return pl.pallas_call(
        _in_proj_kernel,
        grid=(n // tm,),
        in_specs=[pl.BlockSpec((tm, D_MODEL), row),
                  pl.BlockSpec((1, D_MODEL), fix),
                  _layer_block((D_MODEL, IN_COLS), layer, resident=True),
                  pl.BlockSpec((1, HEAD_DIM), fix),
                  pl.BlockSpec((1, HEAD_DIM), fix)],
        out_specs=[pl.BlockSpec((tm // d, d * GROUP_QKV), row) for _, d in ATTN_GROUPS]
        + [pl.BlockSpec((tm, FOURIER_WIDTH), row),
           pl.BlockSpec((tm, POOL_WIDTH), row),
           pl.BlockSpec((tm, GATE_WIDTH), row)],
        out_shape=[jax.ShapeDtypeStruct((n // d, d * GROUP_QKV), BF16) for _, d in ATTN_GROUPS]
        + [jax.ShapeDtypeStruct((n, FOURIER_WIDTH), BF16),
           jax.ShapeDtypeStruct((n, POOL_WIDTH), BF16),
           jax.ShapeDtypeStruct((n, GATE_WIDTH), BF16)],
        scratch_shapes=[pltpu.VMEM((tm, D_MODEL), BF16)]
        + [pltpu.VMEM((tm, LANES), F32)] * (2 * IN_CHUNK // LANES),
        compiler_params=_cparams("parallel"),
        name="in_proj",
    )(x2, g, w, qg, kg)


def _t5_bucket(rel):
    half = N_BUCKETS // 2
    ret = (rel > 0).astype(np.int64) * half
    n = np.abs(rel)
    max_exact = half // 2
    large = max_exact + (np.log(np.maximum(n, 1) / max_exact) / np.log(REL_MAX_DISTANCE / max_exact)
                         * (half - max_exact)).astype(np.int64)
    large = np.minimum(large, half - 1)
    return (ret + np.where(n < max_exact, n, large)).astype(np.int32)


def _attn_bias(rel_bias, group):
    dilation = ATTN_GROUPS[group][1]
    r = np.arange(Q_SUB)[:, None]
    c = np.arange(K_WIN)[None, :]
    heads = rel_bias[:, group * HEADS_PER_GROUP:(group + 1) * HEADS_PER_GROUP].astype(F32)
    off = np.stack([c - variant * HALF_WINDOW - r for variant in range(3)])
    valid = np.abs(off) <= HALF_WINDOW
    bucket = _t5_bucket(np.clip(off, -HALF_WINDOW, HALF_WINDOW) * dilation)
    onehot = (jnp.asarray(bucket)[..., None] == jnp.arange(N_BUCKETS)).astype(F32)
    b = jnp.einsum("vqkn,nh->vhqk", onehot, heads, precision=lax.Precision.HIGHEST)
    return jnp.where(valid[:, None], b, NEG_INF)


def _attn_kernel(qkv_ref, bias_ref, o_ref, lse_ref, vaug_ref, *, seq, res):
    for rr in range(res):
        for h in range(HEADS_PER_GROUP):
            c = (rr * HEADS_PER_GROUP + h) * 2 * HEAD_DIM
            vc = rr * GROUP_QKV + 2 * GROUP_WIDTH + h * HEAD_DIM
            vaug_ref[:, c:c + HEAD_DIM] = qkv_ref[0, :, vc:vc + HEAD_DIM]
            vaug_ref[:, c + HEAD_DIM:c + 2 * HEAD_DIM] = jnp.ones((seq, HEAD_DIM), BF16)

    lane = lax.broadcasted_iota(jnp.int32, (Q_SUB, LANES), 1)
    for rr in range(res):
        for s in range(seq // Q_SUB):
            rows = slice(s * Q_SUB, (s + 1) * Q_SUB)
            start = min(max(s * Q_SUB - HALF_WINDOW, 0), seq - K_WIN)
            variant = (s * Q_SUB - start) // HALF_WINDOW
            keys = slice(start, start + K_WIN)
            lse_tile = jnp.zeros((Q_SUB, LANES), F32)
            for h in range(HEADS_PER_GROUP):
                qc = rr * GROUP_QKV + h * HEAD_DIM
                kc = qc + GROUP_WIDTH
                c = (rr * HEADS_PER_GROUP + h) * 2 * HEAD_DIM
                sc = lax.dot_general(qkv_ref[0, rows, qc:qc + HEAD_DIM], qkv_ref[0, keys, kc:kc + HEAD_DIM],
                                     (((1,), (1,)), ((), ())), preferred_element_type=F32)
                sc = sc + bias_ref[variant, h]
                m = jnp.max(sc, axis=-1, keepdims=True)
                p = jnp.exp(sc - m)
                pv = jnp.dot(p.astype(BF16), vaug_ref[keys, c:c + 2 * HEAD_DIM], preferred_element_type=F32)
                den = pv[:, HEAD_DIM:]
                oc = rr * GROUP_WIDTH + h * HEAD_DIM
                o_ref[0, rows, oc:oc + HEAD_DIM] = (pv[:, :HEAD_DIM] / den).astype(BF16)
                lse_tile = jnp.where(lane == h, m + jnp.log(den), lse_tile)
            lse_ref[0, rows, rr * LANES:(rr + 1) * LANES] = lse_tile


def _attention_group(qkv, bias, group, batch, seq_full):
    dilation = ATTN_GROUPS[group][1]
    seq = seq_full // dilation
    whole = lambda b: (b, 0, 0)
    o, lse = pl.pallas_call(
        functools.partial(_attn_kernel, seq=seq, res=dilation),
        grid=(batch,),
        in_specs=[pl.BlockSpec((1, seq, dilation * GROUP_QKV), whole),
                  _resident((3, HEADS_PER_GROUP, Q_SUB, K_WIN), lambda b: (0, 0, 0, 0))],
        out_specs=[pl.BlockSpec((1, seq, dilation * GROUP_WIDTH), whole),
                   pl.BlockSpec((1, seq, dilation * LANES), whole)],
        out_shape=[jax.ShapeDtypeStruct((batch, seq, dilation * GROUP_WIDTH), BF16),
                   jax.ShapeDtypeStruct((batch, seq, dilation * LANES), F32)],
        scratch_shapes=[pltpu.VMEM((seq, dilation * 2 * GROUP_WIDTH), BF16)],
        compiler_params=_cparams("parallel"),
        name=f"attn_g{group}",
    )(qkv.reshape(batch, seq, dilation * GROUP_QKV), bias)
    return o.reshape(batch * seq, dilation * GROUP_WIDTH), lse.reshape(batch * seq, dilation * LANES)


def _dft_tables(seq):
    c = jnp.arange(HEAD_DIM, dtype=jnp.int32)
    ang_c = ((c[:, None] * c[None, :]) % HEAD_DIM).astype(F32) * (2.0 * math.pi / HEAD_DIM)
    chan = jnp.concatenate([jnp.cos(ang_c), jnp.sin(ang_c)], axis=1) * (HEAD_DIM ** -0.5)
    tiles = seq // 2 // DFT_TILE
    s = jnp.arange(seq, dtype=jnp.int32)

    def angle(k):
        return ((k[:, None] * s[None, :]) % seq).astype(F32) * (2.0 * math.pi / seq)

    ang_j = angle(jnp.arange(DFT_TILE + DFT_EXTRA, dtype=jnp.int32))
    ang_m = angle(jnp.arange(tiles, dtype=jnp.int32) * DFT_TILE)
    cj, sj = jnp.cos(ang_j)[None], jnp.sin(ang_j)[None]
    cm, sm = jnp.cos(ang_m)[:, None, :], jnp.sin(ang_m)[:, None, :]
    cos_t = (cm * cj - sm * sj) * (seq ** -0.5)
    sin_t = (sm * cj + cm * sj) * (seq ** -0.5)
    j = np.arange(DFT_TILE)[:, None]
    flip = (np.arange(DFT_TILE + DFT_EXTRA)[None, :] == DFT_TILE - j).astype(np.float32)
    return chan.astype(BF16), cos_t.astype(BF16), sin_t.astype(BF16), jnp.asarray(flip, BF16)


def _fourier_kernel(u_ref, chan_ref, cos_ref, sin_ref, flip_ref, o_ref, y_ref, *, seq):
    m = pl.program_id(1)

    @pl.when(m == 0)
    def _():
        rows = 512
        for r0 in range(0, seq, rows):
            for g in range(FOURIER_GROUPS):
                y = jnp.dot(u_ref[0, r0:r0 + rows, g * HEAD_DIM:(g + 1) * HEAD_DIM], chan_ref[...],
                            preferred_element_type=F32)
                y_ref[r0:r0 + rows, g * HEAD_DIM:(g + 1) * HEAD_DIM] = y[:, :HEAD_DIM].astype(BF16)
                y_ref[seq + r0:seq + r0 + rows, g * HEAD_DIM:(g + 1) * HEAD_DIM] = y[:, HEAD_DIM:].astype(BF16)

    even = jnp.dot(cos_ref[0], y_ref[0:seq, :], preferred_element_type=F32)
    odd = jnp.dot(sin_ref[0], y_ref[seq:2 * seq, :], preferred_element_type=F32)
    lo = pl.multiple_of(m * DFT_TILE, DFT_TILE)
    o_ref[0, pl.ds(lo, DFT_TILE), :] = (even - odd)[:DFT_TILE].astype(BF16)
    mirrored = jnp.dot(flip_ref[...], (even + odd).astype(BF16), preferred_element_type=F32)
    hi = pl.multiple_of(seq - lo - DFT_TILE, DFT_TILE)
    o_ref[0, pl.ds(hi, DFT_TILE), :] = mirrored.astype(BF16)


def _fourier(uf, tables, batch, seq):
    chan, cos_t, sin_t, flip = tables
    u3 = uf.reshape(batch, seq, FOURIER_WIDTH)
    rows = DFT_TILE + DFT_EXTRA
    out = pl.pallas_call(
        functools.partial(_fourier_kernel, seq=seq),
        grid=(batch, seq // 2 // DFT_TILE),
        in_specs=[pl.BlockSpec((1, seq, FOURIER_WIDTH), lambda b, m: (b, 0, 0)),
                  pl.BlockSpec((HEAD_DIM, 2 * HEAD_DIM), lambda b, m: (0, 0)),
                  pl.BlockSpec((1, rows, seq), lambda b, m: (m, 0, 0)),
                  pl.BlockSpec((1, rows, seq), lambda b, m: (m, 0, 0)),
                  pl.BlockSpec((DFT_TILE, rows), lambda b, m: (0, 0))],
        out_specs=pl.BlockSpec((1, seq, FOURIER_WIDTH), lambda b, m: (b, 0, 0)),
        out_shape=jax.ShapeDtypeStruct((batch, seq, FOURIER_WIDTH), BF16),
        scratch_shapes=[pltpu.VMEM((2 * seq, FOURIER_WIDTH), BF16)],
        compiler_params=_cparams("parallel", "arbitrary"),
        name="fourier",
    )(u3, chan, cos_t, sin_t, flip)
    return out.reshape(batch * seq, FOURIER_WIDTH)


def _pool_kernel(u_ref, w_ref, scale_ref, o_ref, pad_ref, *, seq):
    rows = 512
    zeros = jnp.zeros((POOL_PAD, HEAD_DIM), F32)
    pad_ref[0:POOL_PAD, :] = zeros
    pad_ref[POOL_PAD + seq:2 * POOL_PAD + seq, :] = zeros
    for g, window in enumerate(POOL_WINDOWS):
        cols = slice(g * HEAD_DIM, (g + 1) * HEAD_DIM)
        half = window // 2
        pad_ref[POOL_PAD:POOL_PAD + seq, :] = u_ref[0, :, cols].astype(F32)
        for r0 in range(0, seq, rows):
            base = POOL_PAD + r0
            acc = pad_ref[base - half:base - half + rows, :]
            for off in range(-half + 1, half):
                acc = acc + pad_ref[base + off:base + off + rows, :]
            pos = r0 + lax.broadcasted_iota(jnp.int32, (rows, 1), 0)
            cnt = jnp.minimum(pos + half - 1, seq - 1) - jnp.maximum(pos - half, 0) + 1
            d = acc / cnt.astype(F32) - pad_ref[base:base + rows, :]
            y = jnp.dot(d.astype(BF16), w_ref[g], preferred_element_type=F32) * scale_ref[:, cols]
            o_ref[0, r0:r0 + rows, cols] = y.astype(BF16)


def _pool(up, w, layer, scale, batch, seq):
    u3 = up.reshape(batch, seq, POOL_WIDTH)
    out = pl.pallas_call(
        functools.partial(_pool_kernel, seq=seq),
        grid=(batch,),
        in_specs=[pl.BlockSpec((1, seq, POOL_WIDTH), lambda b: (b, 0, 0)),
                  _layer_block((len(POOL_WINDOWS), HEAD_DIM, HEAD_DIM), layer),
                  pl.BlockSpec((1, POOL_WIDTH), lambda b: (0, 0))],
        out_specs=pl.BlockSpec((1, seq, POOL_WIDTH), lambda b: (b, 0, 0)),
        out_shape=jax.ShapeDtypeStruct((batch, seq, POOL_WIDTH), BF16),
        scratch_shapes=[pltpu.VMEM((seq + 2 * POOL_PAD, HEAD_DIM), F32)],
        compiler_params=_cparams("parallel"),
        name="pool",
    )(u3, w, scale)
    return out.reshape(batch * seq, POOL_WIDTH)


def _merge_kernel(x_ref, o1_ref, o2_ref, o3_ref, l1_ref, l2_ref, l3_ref, four_ref, pool_ref, gate_ref,
                  wba_ref, wbf_ref, wbp_ref, wout_ref, gffn_ref, xo_ref, *rest):
    n_dilated = len(ATTN_GROUPS) - 1
    n_scratch = n_dilated * HEADS_PER_GROUP + n_dilated + 2
    h_refs, scratch = rest[:-n_scratch], rest[-n_scratch:]
    oslabs = scratch[:n_dilated * HEADS_PER_GROUP]
    lslabs = scratch[n_dilated * HEADS_PER_GROUP:-2]
    attn_ref, acc_ref = scratch[-2:]
    tm = x_ref.shape[0]
    o_refs = (o1_ref, o2_ref, o3_ref)
    l_refs = (l1_ref, l2_ref, l3_ref)
    f = jnp.dot(four_ref[...], wbf_ref[...], preferred_element_type=F32)
    p = jnp.dot(pool_ref[...], wbp_ref[...], preferred_element_type=F32)
    acc_ref[...] = (gate_ref[:, D_MODEL:2 * D_MODEL].astype(F32) * f
                    + gate_ref[:, 2 * D_MODEL:3 * D_MODEL].astype(F32) * p)
    dilated = [(g, d) for g, (_, d) in enumerate(ATTN_GROUPS) if d > 1]
    for g, d in dilated:
        for r in range(d):
            lslabs[g - 1][pl.ds(r, tm // d, stride=d), :] = l_refs[g][:, r * LANES:(r + 1) * LANES]
    lses = [l1_ref[...]] + [ref[...] for ref in lslabs]
    top = functools.reduce(jnp.maximum, lses)
    es = [jnp.exp(l - top) for l in lses]
    inv = 1.0 / functools.reduce(lambda a, b: a + b, es)
    weights = [e * inv for e in es]
    for j in range(HEADS_PER_GROUP):
        cols = slice(j * HEAD_DIM, (j + 1) * HEAD_DIM)
        for g, d in dilated:
            slab = oslabs[(g - 1) * HEADS_PER_GROUP + j]
            for r in range(d):
                lo = r * GROUP_WIDTH + j * HEAD_DIM
                slab[pl.ds(r, tm // d, stride=d), :] = o_refs[g][:, lo:lo + HEAD_DIM].astype(F32)
        outs = [o1_ref[:, cols].astype(F32)] + [oslabs[(g - 1) * HEADS_PER_GROUP + j][...] for g, _ in dilated]
        attn_ref[:, cols] = sum(w[:, j:j + 1] * o for w, o in zip(weights, outs)).astype(BF16)
    a = jnp.dot(attn_ref[...], wba_ref[...], preferred_element_type=F32)
    merged = gate_ref[:, 0:D_MODEL].astype(F32) * a + acc_ref[...]
    xn = x_ref[...] + jnp.dot(merged.astype(BF16), wout_ref[...], preferred_element_type=F32)
    xo_ref[...] = xn
    if h_refs:
        h_refs[0][...] = _rms(xn, gffn_ref[...]).astype(BF16)


def _merge(x2, os_, ls_, four, pool, gates, wba, wbf, wbp, wout, layer, gffn, with_h, tm=512):
    n = x2.shape[0]
    row = lambda i: (i, 0)
    fix = lambda i: (0, 0)
    out_specs = [pl.BlockSpec((tm, D_MODEL), row)]
    out_shape = [jax.ShapeDtypeStruct((n, D_MODEL), F32)]
    if with_h:
        out_specs.append(pl.BlockSpec((tm, D_MODEL), row))
        out_shape.append(jax.ShapeDtypeStruct((n, D_MODEL), BF16))
    return pl.pallas_call(
        _merge_kernel,
        grid=(n // tm,),
        in_specs=[pl.BlockSpec((tm, D_MODEL), row)]
        + [pl.BlockSpec((tm // d, d * GROUP_WIDTH), row) for _, d in ATTN_GROUPS]
        + [pl.BlockSpec((tm // d, d * LANES), row) for _, d in ATTN_GROUPS]
        + [pl.BlockSpec((tm, FOURIER_WIDTH), row), pl.BlockSpec((tm, POOL_WIDTH), row),
           pl.BlockSpec((tm, GATE_WIDTH), row)]
        + [_layer_block((GROUP_WIDTH, D_MODEL), layer)] * 3
        + [_layer_block((D_MODEL, D_MODEL), layer), pl.BlockSpec((1, D_MODEL), fix)],
        out_specs=out_specs,
        out_shape=out_shape,
        scratch_shapes=[pltpu.VMEM((tm, LANES), F32)] * ((len(ATTN_GROUPS) - 1) * (HEADS_PER_GROUP + 1))
        + [pltpu.VMEM((tm, GROUP_WIDTH), BF16), pltpu.VMEM((tm, D_MODEL), F32)],
        compiler_params=_cparams("parallel"),
        name="merge",
    )(x2, *os_, *ls_, four, pool, gates, wba, wbf, wbp, wout, gffn)


FFN_CHUNK = 256


def _silu(a):
    return a * (0.5 * jnp.tanh(0.5 * a) + 0.5)


def _ffn_kernel(x_ref, h_ref, w1_ref, w3_ref, w2_ref, o_ref, g_ref):
    h = h_ref[...]
    ffn = w1_ref.shape[1]
    for c0 in range(0, ffn, FFN_CHUNK):
        a = jnp.dot(h, w1_ref[:, c0:c0 + FFN_CHUNK], preferred_element_type=F32)
        b = jnp.dot(h, w3_ref[:, c0:c0 + FFN_CHUNK], preferred_element_type=F32)
        g_ref[:, c0:c0 + FFN_CHUNK] = (_silu(a) * b).astype(BF16)
    o_ref[...] = x_ref[...] + jnp.dot(g_ref[...], w2_ref[...], preferred_element_type=F32)


def _dense_ffn(x2, h, w1, w3, w2, layer, tm=512):
    n = x2.shape[0]
    ffn = w1.shape[2]
    row = lambda i: (i, 0)
    return pl.pallas_call(
        _ffn_kernel,
        grid=(n // tm,),
        in_specs=[pl.BlockSpec((tm, D_MODEL), row), pl.BlockSpec((tm, D_MODEL), row),
                  _layer_block((D_MODEL, ffn), layer, resident=True),
                  _layer_block((D_MODEL, ffn), layer, resident=True),
                  _layer_block((ffn, D_MODEL), layer, resident=True)],
        out_specs=pl.BlockSpec((tm, D_MODEL), row),
        out_shape=jax.ShapeDtypeStruct((n, D_MODEL), F32),
        scratch_shapes=[pltpu.VMEM((tm, ffn), BF16)],
        compiler_params=_cparams("parallel"),
        name="dense_ffn",
    )(x2, h, w1, w3, w2)


R_E0, R_E1, R_G0, R_G1, R_RANK0, R_RANK1 = 0, 1, 2, 3, 4, 5


def _router_kernel(x_ref, g_ref, rw_ref, rb_ref, meta_ref, metat_ref, cnt_ref, *, tm):
    h = _rms(x_ref[...], g_ref[...])
    logits = jnp.dot(h.astype(BF16), rw_ref[...], preferred_element_type=F32) + rb_ref[...]
    lane = lax.broadcasted_iota(jnp.int32, (tm, LANES), 1)
    logits = jnp.where(lane < N_EXPERTS, logits, -jnp.inf)
    v0 = jnp.max(logits, axis=-1, keepdims=True)
    e0 = jnp.min(jnp.where(logits == v0, lane, LANES), axis=-1, keepdims=True)
    rest = jnp.where(lane == e0, -jnp.inf, logits)
    v1 = jnp.max(rest, axis=-1, keepdims=True)
    e1 = jnp.min(jnp.where(rest == v1, lane, LANES), axis=-1, keepdims=True)
    t = jnp.exp(v1 - v0)
    gate0 = 1.0 / (1.0 + t)
    gate1 = t / (1.0 + t)

    picked = jnp.logical_or(lane == e0, lane == e1)
    tri = (lax.broadcasted_iota(jnp.int32, (tm, tm), 1) < lax.broadcasted_iota(jnp.int32, (tm, tm), 0))
    before = jnp.dot(tri.astype(BF16), picked.astype(BF16), preferred_element_type=F32)
    rank0 = jnp.sum(jnp.where(lane == e0, before, 0.0), axis=-1, keepdims=True)
    rank1 = jnp.sum(jnp.where(lane == e1, before, 0.0), axis=-1, keepdims=True)

    meta = jnp.zeros((tm, LANES), F32)
    for ln, val in ((R_E0, e0.astype(F32)), (R_E1, e1.astype(F32)), (R_G0, gate0), (R_G1, gate1),
                    (R_RANK0, rank0), (R_RANK1, rank1)):
        meta = jnp.where(lane == ln, val, meta)
    meta_ref[...] = meta
    metat_ref[0] = jnp.transpose(meta)[:8]
    cnt_ref[0] = jnp.sum(picked.astype(F32), axis=0, keepdims=True)


def _router(x2, g, rw, rb, tm):
    n = x2.shape[0]
    rw_p = jnp.zeros((D_MODEL, LANES), BF16).at[:, :N_EXPERTS].set(rw.astype(BF16))
    rb_p = jnp.zeros((1, LANES), F32).at[0, :N_EXPERTS].set(rb.astype(F32))
    return pl.pallas_call(
        functools.partial(_router_kernel, tm=tm),
        grid=(n // tm,),
        in_specs=[pl.BlockSpec((tm, D_MODEL), lambda i: (i, 0)),
                  pl.BlockSpec((1, D_MODEL), lambda i: (0, 0)),
                  pl.BlockSpec((D_MODEL, LANES), lambda i: (0, 0)),
                  pl.BlockSpec((1, LANES), lambda i: (0, 0))],
        out_specs=[pl.BlockSpec((tm, LANES), lambda i: (i, 0)),
                   pl.BlockSpec((1, 8, tm), lambda i: (i, 0, 0)),
                   pl.BlockSpec((1, 1, LANES), lambda i: (i, 0, 0))],
        out_shape=[jax.ShapeDtypeStruct((n, LANES), F32),
                   jax.ShapeDtypeStruct((n // tm, 8, tm), F32),
                   jax.ShapeDtypeStruct((n // tm, 1, LANES), F32)],
        compiler_params=_cparams("parallel"),
        name="router",
    )(x2, g, rw_p, rb_p)


MOE_TILE = 512
RUN_ALIGN = 16
RUN_CHUNKS = (512, 256, 128, 64, 32, 16)
SORT_ROWS = -(-(MOE_TILE * TOP_K + N_EXPERTS * (RUN_ALIGN - 1)) // LANES) * LANES


def _run_copies(i, loc_ref, glob_ref, len_ref, make_copy, action):
    for e in range(N_EXPERTS):
        n = len_ref[i * N_EXPERTS + e]
        lo = loc_ref[i * N_EXPERTS + e]
        go = glob_ref[i * N_EXPERTS + e]
        for size in RUN_CHUNKS:
            done = n & ~(2 * size - 1)

            @pl.when((n & size) != 0)
            def _():
                action(make_copy(pl.multiple_of(lo + done, RUN_ALIGN), pl.multiple_of(go + done, RUN_ALIGN), size))


def _run_base(i, loc_ref, expert):
    base = jnp.zeros_like(expert)
    for e in range(N_EXPERTS):
        base = jnp.where(expert == e, loc_ref[i * N_EXPERTS + e], base)
    return base


def _dispatch_kernel(loc_ref, glob_ref, len_ref, x_ref, g_ref, metat_ref, xs_hbm, sorted_ref, zero_ref, sem):
    i = pl.program_id(0)
    n_tiles = pl.num_programs(0)

    @pl.when(i == 0)
    def _():
        zero_ref[...] = jnp.zeros_like(zero_ref)

        def make_copy(lo, go, size):
            return pltpu.make_async_copy(zero_ref.at[pl.ds(lo, size), :], xs_hbm.at[pl.ds(go, size), :], sem.at[2])

        _run_copies(n_tiles, loc_ref, glob_ref, len_ref, make_copy, lambda c: c.start())
        _run_copies(n_tiles, loc_ref, glob_ref, len_ref, make_copy, lambda c: c.wait())
        last = n_tiles * N_EXPERTS + N_EXPERTS - 1
        first_unused = (glob_ref[last] + len_ref[last]) // EXPERT_BLOCK

        def clear(j, carry):
            block = make_copy(0, pl.multiple_of(j * EXPERT_BLOCK, EXPERT_BLOCK), EXPERT_BLOCK)
            block.start()
            block.wait()
            return carry

        lax.fori_loop(first_unused, xs_hbm.shape[0] // EXPERT_BLOCK, clear, 0)

    h = _rms(x_ref[...], g_ref[...]).astype(BF16)
    rec = metat_ref[0]
    slot0 = _run_base(i, loc_ref, rec[R_E0:R_E0 + 1].astype(jnp.int32)) + rec[R_RANK0:R_RANK0 + 1].astype(jnp.int32)
    slot1 = _run_base(i, loc_ref, rec[R_E1:R_E1 + 1].astype(jnp.int32)) + rec[R_RANK1:R_RANK1 + 1].astype(jnp.int32)
    rows = lax.broadcasted_iota(jnp.int32, (SORT_ROWS, MOE_TILE), 0)
    perm = jnp.logical_or(rows == slot0, rows == slot1).astype(BF16)
    sorted_ref[i % 2] = jnp.dot(perm, h, preferred_element_type=F32).astype(BF16)

    def copies(tile, action):
        def make_copy(lo, go, size):
            return pltpu.make_async_copy(sorted_ref.at[tile % 2, pl.ds(lo, size), :],
                                         xs_hbm.at[pl.ds(go, size), :], sem.at[tile % 2])
        _run_copies(tile, loc_ref, glob_ref, len_ref, make_copy, action)

    copies(i, lambda c: c.start())

    @pl.when(i > 0)
    def _():
        copies(i - 1, lambda c: c.wait())

    @pl.when(i == n_tiles - 1)
    def _():
        copies(i, lambda c: c.wait())


def _dispatch(x2, g, metat, loc, glob, plen, total):
    n = x2.shape[0]
    return pl.pallas_call(
        _dispatch_kernel,
        grid_spec=pltpu.PrefetchScalarGridSpec(
            num_scalar_prefetch=3,
            grid=(n // MOE_TILE,),
            in_specs=[pl.BlockSpec((MOE_TILE, D_MODEL), lambda i, *_: (i, 0)),
                      pl.BlockSpec((1, D_MODEL), lambda i, *_: (0, 0)),
                      pl.BlockSpec((1, 8, MOE_TILE), lambda i, *_: (i, 0, 0))],
            out_specs=pl.BlockSpec(memory_space=pl.ANY),
            scratch_shapes=[pltpu.VMEM((2, SORT_ROWS, D_MODEL), BF16), pltpu.VMEM((EXPERT_BLOCK, D_MODEL), BF16),
                            pltpu.SemaphoreType.DMA((3,))]),
        out_shape=jax.ShapeDtypeStruct((total, D_MODEL), BF16),
        compiler_params=_cparams("arbitrary"),
        name="moe_dispatch",
    )(loc, glob, plen, x2, g, metat)


MOE_CHUNK = 512


def _expert_kernel(blk_e_ref, n_used_ref, xs_ref, w1_ref, w3_ref, w2_ref, ys_ref, act_ref):
    j = pl.program_id(0)

    @pl.when(j < n_used_ref[0])
    def _():
        h = xs_ref[...]
        ffn = w1_ref.shape[2]
        for c0 in range(0, ffn, MOE_CHUNK):
            a = jnp.dot(h, w1_ref[0, :, c0:c0 + MOE_CHUNK], preferred_element_type=F32)
            b = jnp.dot(h, w3_ref[0, :, c0:c0 + MOE_CHUNK], preferred_element_type=F32)
            act_ref[:, c0:c0 + MOE_CHUNK] = (_silu(a) * b).astype(BF16)
        ys_ref[...] = jnp.dot(act_ref[...], w2_ref[0], preferred_element_type=F32).astype(BF16)

    @pl.when(j >= n_used_ref[0])
    def _():
        ys_ref[...] = jnp.zeros_like(ys_ref)


def _experts(xs, w1, w3, w2, layer, blk_e, n_used):
    total = xs.shape[0]
    ffn = w1.shape[3]
    n_blocks = total // EXPERT_BLOCK
    expert = lambda j, be, nu: (layer, be[j], 0, 0)
    return pl.pallas_call(
        _expert_kernel,
        grid_spec=pltpu.PrefetchScalarGridSpec(
            num_scalar_prefetch=2,
            grid=(n_blocks,),
            in_specs=[pl.BlockSpec((EXPERT_BLOCK, D_MODEL), lambda j, be, nu: (jnp.minimum(j, nu[0] - 1), 0)),
                      _resident((None, 1, D_MODEL, ffn), expert),
                      _resident((None, 1, D_MODEL, ffn), expert),
                      _resident((None, 1, ffn, D_MODEL), expert)],
            out_specs=pl.BlockSpec((EXPERT_BLOCK, D_MODEL), lambda j, be, nu: (j, 0)),
            scratch_shapes=[pltpu.VMEM((EXPERT_BLOCK, ffn), BF16)]),
        out_shape=jax.ShapeDtypeStruct((total, D_MODEL), BF16),
        compiler_params=_cparams("arbitrary"),
        name="moe_experts",
    )(blk_e, n_used, xs, w1, w3, w2)


def _combine_kernel(loc_ref, glob_ref, len_ref, x_ref, meta_ref, ys_hbm, o_ref, buf_ref, sem):
    i = pl.program_id(0)

    def copies(tile, action):
        def make_copy(lo, go, size):
            return pltpu.make_async_copy(ys_hbm.at[pl.ds(go, size), :],
                                         buf_ref.at[tile % 2, pl.ds(lo, size), :], sem.at[tile % 2])
        _run_copies(tile, loc_ref, glob_ref, len_ref, make_copy, action)

    @pl.when(i == 0)
    def _():
        copies(i, lambda c: c.start())

    @pl.when(i + 1 < pl.num_programs(0))
    def _():
        copies(i + 1, lambda c: c.start())

    rec = meta_ref[...]
    slot0 = _run_base(i, loc_ref, rec[:, R_E0:R_E0 + 1].astype(jnp.int32)) + rec[:, R_RANK0:R_RANK0 + 1].astype(jnp.int32)
    slot1 = _run_base(i, loc_ref, rec[:, R_E1:R_E1 + 1].astype(jnp.int32)) + rec[:, R_RANK1:R_RANK1 + 1].astype(jnp.int32)
    cols = lax.broadcasted_iota(jnp.int32, (MOE_TILE, SORT_ROWS), 1)
    weight = jnp.where(cols == slot0, rec[:, R_G0:R_G0 + 1],
                       jnp.where(cols == slot1, rec[:, R_G1:R_G1 + 1], 0.0)).astype(BF16)
    copies(i, lambda c: c.wait())
    filled = loc_ref[i * N_EXPERTS + N_EXPERTS - 1] + len_ref[i * N_EXPERTS + N_EXPERTS - 1]
    rows = lax.broadcasted_iota(jnp.int32, (SORT_ROWS, 1), 0)
    picked = jnp.where(rows < filled, buf_ref[i % 2], jnp.zeros((), BF16))
    o_ref[...] = x_ref[...] + jnp.dot(weight, picked, preferred_element_type=F32)


def _combine(x2, meta, ys, loc, glob, plen):
    n = x2.shape[0]
    return pl.pallas_call(
        _combine_kernel,
        grid_spec=pltpu.PrefetchScalarGridSpec(
            num_scalar_prefetch=3,
            grid=(n // MOE_TILE,),
            in_specs=[pl.BlockSpec((MOE_TILE, D_MODEL), lambda i, *_: (i, 0)),
                      pl.BlockSpec((MOE_TILE, LANES), lambda i, *_: (i, 0)),
                      pl.BlockSpec(memory_space=pl.ANY)],
            out_specs=pl.BlockSpec((MOE_TILE, D_MODEL), lambda i, *_: (i, 0)),
            scratch_shapes=[pltpu.VMEM((2, SORT_ROWS, D_MODEL), BF16), pltpu.SemaphoreType.DMA((2,))]),
        out_shape=jax.ShapeDtypeStruct((n, D_MODEL), F32),
        compiler_params=_cparams("arbitrary"),
        name="moe_combine",
    )(loc, glob, plen, x2, meta, ys)


def _moe(x2, g, rw, rb, w1, w3, w2, layer):
    n = x2.shape[0]
    n_tiles = n // MOE_TILE
    meta, metat, cnt = _router(x2, g, rw, rb, MOE_TILE)
    cnt = cnt[:, 0, :N_EXPERTS].astype(jnp.int32)
    plen = (cnt + RUN_ALIGN - 1) // RUN_ALIGN * RUN_ALIGN
    loc = jnp.cumsum(plen, axis=1) - plen
    filled = jnp.sum(plen, axis=0)
    padded = (filled + EXPERT_BLOCK - 1) // EXPERT_BLOCK * EXPERT_BLOCK
    pend = jnp.cumsum(padded)
    glob = (pend - padded)[None, :] + jnp.cumsum(plen, axis=0) - plen
    loc = jnp.concatenate([loc, jnp.zeros((1, N_EXPERTS), jnp.int32)])
    glob = jnp.concatenate([glob, (pend - padded + filled)[None, :]])
    plen = jnp.concatenate([plen, (padded - filled)[None, :]])
    n_blocks = -(-(n * TOP_K + n_tiles * N_EXPERTS * (RUN_ALIGN - 1)) // EXPERT_BLOCK) + N_EXPERTS
    blk_start = jnp.arange(n_blocks, dtype=jnp.int32) * EXPERT_BLOCK
    blk_e = jnp.minimum(jnp.sum(blk_start[:, None] >= pend[None, :], axis=1), N_EXPERTS - 1).astype(jnp.int32)
    n_used = (pend[-1] // EXPERT_BLOCK).astype(jnp.int32).reshape(1)
    loc, glob, plen = (a.reshape(-1).astype(jnp.int32) for a in (loc, glob, plen))
    xs = _dispatch(x2, g, metat, loc, glob, plen, n_blocks * EXPERT_BLOCK)
    ys = _experts(xs, w1, w3, w2, layer, blk_e, n_used)
    return _combine(x2, meta, ys, loc, glob, plen)


def kernel(x, norm_mix_g, w_in, q_norm_g, k_norm_g, rel_bias, pool_w, pool_scale, w_branch_attn,
           w_branch_fourier, w_branch_pool, w_out, norm_ffn_g, ffn_w1, ffn_w3, ffn_w2,
           router_w, router_b, moe_w1, moe_w3, moe_w2):
    batch, seq, _ = x.shape
    depth = w_in.shape[0]
    n = batch * seq
    x2 = x.reshape(n, D_MODEL)
    biases = [_attn_bias(rel_bias, g) for g in range(len(ATTN_GROUPS))]
    dft = _dft_tables(seq)
    row = lambda v: v.reshape(1, -1).astype(F32)
    (w_in, pool_w, w_branch_attn, w_branch_fourier, w_branch_pool, w_out, ffn_w1, ffn_w3, ffn_w2,
     moe_w1, moe_w3, moe_w2) = (w.astype(BF16) for w in (
         w_in, pool_w, w_branch_attn, w_branch_fourier, w_branch_pool, w_out, ffn_w1, ffn_w3, ffn_w2,
         moe_w1, moe_w3, moe_w2))
    for layer in range(depth):
        *qkvs, uf, up, gates = _in_proj(x2, row(norm_mix_g[layer]), w_in, layer,
                                        row(q_norm_g[layer]) * (HEAD_DIM ** -0.5), row(k_norm_g[layer]))
        os_, ls_ = [], []
        for g in range(len(ATTN_GROUPS)):
            o, lse = _attention_group(qkvs[g], biases[g], g, batch, seq)
            os_.append(o)
            ls_.append(lse)
        four = _fourier(uf, dft, batch, seq)
        pool = _pool(up, pool_w, layer, row(pool_scale[layer]), batch, seq)
        dense = layer % 2 == 0
        j = layer // 2
        outs = _merge(x2, os_, ls_, four, pool, gates, w_branch_attn, w_branch_fourier, w_branch_pool,
                      w_out, layer, row(norm_ffn_g[layer]), with_h=dense)
        if dense:
            x2 = _dense_ffn(outs[0], outs[1], ffn_w1, ffn_w3, ffn_w2, j)
        else:
            x2 = _moe(outs[0], row(norm_ffn_g[layer]), router_w[j], router_b[j], moe_w1, moe_w3, moe_w2, j)
    return x2.reshape(batch, seq, D_MODEL)
```

```python
import functools
import math

import numpy as np
import jax
import jax.numpy as jnp
from jax import lax
from jax.experimental import pallas as pl
from jax.experimental.pallas import tpu as pltpu

F32 = jnp.float32
BF16 = jnp.bfloat16

D_MODEL = 1024
HEAD_DIM = 128
ATTN_GROUPS = ((128, 1), (512, 4), (2048, 16))
HEADS_PER_GROUP = 4
N_ATTN_HEADS = HEADS_PER_GROUP * len(ATTN_GROUPS)
ATTN_WIDTH = N_ATTN_HEADS * HEAD_DIM
GROUP_WIDTH = HEADS_PER_GROUP * HEAD_DIM
QKV_WIDTH = 3 * ATTN_WIDTH
GROUP_QKV = 3 * GROUP_WIDTH
N_BUCKETS = 32
REL_MAX_DISTANCE = 1024
FOURIER_GROUPS = 4
FOURIER_WIDTH = 512
POOL_WINDOWS = (2, 4, 8, 16)
POOL_WIDTH = 512
N_BRANCHES = 3
GATE_WIDTH = N_BRANCHES * D_MODEL
IN_COLS = QKV_WIDTH + FOURIER_WIDTH + POOL_WIDTH + GATE_WIDTH
N_EXPERTS = 8
TOP_K = 2
EXPERT_BLOCK = 512
RMS_EPS = 1e-6
NEG_INF = -1e30

LANES = 128
HALF_WINDOW = 64
Q_SUB = 128
K_WIN = Q_SUB + 2 * HALF_WINDOW
POOL_PAD = 8
DFT_TILE = 512
DFT_EXTRA = 8
VMEM_LIMIT = 56 * 1024 * 1024


def _cparams(*sem):
    return pltpu.CompilerParams(dimension_semantics=sem, vmem_limit_bytes=VMEM_LIMIT)


def _resident(shape, index_map):
    return pl.BlockSpec(shape, index_map, pipeline_mode=pl.Buffered(1))


def _layer_block(shape, layer, resident=False):
    index_map = lambda *_: (layer,) + (0,) * len(shape)
    make = _resident if resident else pl.BlockSpec
    return make((None,) + tuple(shape), index_map)


def _rms(x, gain):
    return x * lax.rsqrt(jnp.mean(x * x, axis=-1, keepdims=True) + RMS_EPS) * gain


IN_CHUNK = 512
EPI_ROWS = 128


def _in_proj_kernel(x_ref, g_ref, w_ref, qg_ref, kg_ref, qkv0_ref, qkv1_ref, qkv2_ref, uf_ref, up_ref, gate_ref,
                    h_ref, *slab_refs):
    tm = x_ref.shape[0]
    qkv_refs = (qkv0_ref, qkv1_ref, qkv2_ref)
    h_ref[...] = _rms(x_ref[...], g_ref[...]).astype(BF16)
    pieces = IN_CHUNK // LANES
    for c in range(IN_COLS // IN_CHUNK):
        col = c * IN_CHUNK
        acc = jnp.dot(h_ref[...], w_ref[:, col:col + IN_CHUNK], preferred_element_type=F32)
        slabs = slab_refs[(c % 2) * pieces:(c % 2 + 1) * pieces]
        for hh in range(pieces):
            slabs[hh][...] = acc[:, hh * LANES:(hh + 1) * LANES]
        if col < QKV_WIDTH:
            kind, group = divmod(c, len(ATTN_GROUPS))
            dilation = ATTN_GROUPS[group][1]
            out_ref = qkv_refs[group]
            for hh in range(HEADS_PER_GROUP):
                lo = kind * GROUP_WIDTH + hh * HEAD_DIM
                for r in range(dilation):
                    for r0 in range(0, tm // dilation, EPI_ROWS):
                        n_rows = min(EPI_ROWS, tm // dilation)
                        if dilation == 1:
                            a = slabs[hh][r0:r0 + n_rows, :]
                        else:
                            a = slabs[hh][pl.ds(r0 * dilation + r, n_rows, stride=dilation), :]
                        if kind < 2:
                            a = _rms(a, qg_ref[...] if kind == 0 else kg_ref[...])
                        c0 = r * GROUP_QKV + lo
                        out_ref[r0:r0 + n_rows, c0:c0 + HEAD_DIM] = a.astype(BF16)
        elif col < QKV_WIDTH + FOURIER_WIDTH + POOL_WIDTH:
            out_ref = uf_ref if col < QKV_WIDTH + FOURIER_WIDTH else up_ref
            for hh in range(pieces):
                for r0 in range(0, tm, EPI_ROWS):
                    out_ref[r0:r0 + EPI_ROWS, hh * LANES:(hh + 1) * LANES] = slabs[hh][r0:r0 + EPI_ROWS, :].astype(BF16)
        else:
            lo = col - (IN_COLS - GATE_WIDTH)
            for hh in range(pieces):
                for r0 in range(0, tm, EPI_ROWS):
                    gate = 0.5 * jnp.tanh(0.5 * slabs[hh][r0:r0 + EPI_ROWS, :]) + 0.5
                    gate_ref[r0:r0 + EPI_ROWS, lo + hh * LANES:lo + (hh + 1) * LANES] = gate.astype(BF16)


def _in_proj(x2, g, w, layer, qg, kg, tm=512):
    n = x2.shape[0]
    row = lambda i: (i, 0)
    fix = lambda i: (0, 0)
    return pl.pallas_call(
        _in_proj_kernel,
        grid=(n // tm,),
        in_specs=[pl.BlockSpec((tm, D_MODEL), row),
                  pl.BlockSpec((1, D_MODEL), fix),
                  _layer_block((D_MODEL, IN_COLS), layer, resident=True),
                  pl.BlockSpec((1, HEAD_DIM), fix),
                  pl.BlockSpec((1, HEAD_DIM), fix)],
        out_specs=[pl.BlockSpec((tm // d, d * GROUP_QKV), row) for _, d in ATTN_GROUPS]
        + [pl.BlockSpec((tm, FOURIER_WIDTH), row),
           pl.BlockSpec((tm, POOL_WIDTH), row),
           pl.BlockSpec((tm, GATE_WIDTH), row)],
        out_shape=[jax.ShapeDtypeStruct((n // d, d * GROUP_QKV), BF16) for _, d in ATTN_GROUPS]
        + [jax.ShapeDtypeStruct((n, FOURIER_WIDTH), BF16),
           jax.ShapeDtypeStruct((n, POOL_WIDTH), BF16),
           jax.ShapeDtypeStruct((n, GATE_WIDTH), BF16)],
        scratch_shapes=[pltpu.VMEM((tm, D_MODEL), BF16)]
        + [pltpu.VMEM((tm, LANES), F32)] * (2 * IN_CHUNK // LANES),
        compiler_params=_cparams("parallel"),
        name="in_proj",
    )(x2, g, w, qg, kg)


def _t5_bucket(rel):
    half = N_BUCKETS // 2
    ret = (rel > 0).astype(np.int64) * half
    n = np.abs(rel)
    max_exact = half // 2
    large = max_exact + (np.log(np.maximum(n, 1) / max_exact) / np.log(REL_MAX_DISTANCE / max_exact)
                         * (half - max_exact)).astype(np.int64)
    large = np.minimum(large, half - 1)
    return (ret + np.where(n < max_exact, n, large)).astype(np.int32)


def _attn_bias(rel_bias, group):
    dilation = ATTN_GROUPS[group][1]
    r = np.arange(Q_SUB)[:, None]
    c = np.arange(K_WIN)[None, :]
    heads = rel_bias[:, group * HEADS_PER_GROUP:(group + 1) * HEADS_PER_GROUP].astype(F32)
    off = np.stack([c - variant * HALF_WINDOW - r for variant in range(3)])
    valid = np.abs(off) <= HALF_WINDOW
    bucket = _t5_bucket(np.clip(off, -HALF_WINDOW, HALF_WINDOW) * dilation)
    onehot = (jnp.asarray(bucket)[..., None] == jnp.arange(N_BUCKETS)).astype(F32)
    b = jnp.einsum("vqkn,nh->vhqk", onehot, heads, precision=lax.Precision.HIGHEST)
    return jnp.where(valid[:, None], b, NEG_INF)


def _attn_kernel(qkv_ref, bias_ref, o_ref, lse_ref, vaug_ref, *, seq, res):
    for rr in range(res):
        for h in range(HEADS_PER_GROUP):
            c = (rr * HEADS_PER_GROUP + h) * 2 * HEAD_DIM
            vc = rr * GROUP_QKV + 2 * GROUP_WIDTH + h * HEAD_DIM
            vaug_ref[:, c:c + HEAD_DIM] = qkv_ref[0, :, vc:vc + HEAD_DIM]
            vaug_ref[:, c + HEAD_DIM:c + 2 * HEAD_DIM] = jnp.ones((seq, HEAD_DIM), BF16)

    lane = lax.broadcasted_iota(jnp.int32, (Q_SUB, LANES), 1)
    for rr in range(res):
        for s in range(seq // Q_SUB):
            rows = slice(s * Q_SUB, (s + 1) * Q_SUB)
            start = min(max(s * Q_SUB - HALF_WINDOW, 0), seq - K_WIN)
            variant = (s * Q_SUB - start) // HALF_WINDOW
            keys = slice(start, start + K_WIN)
            lse_tile = jnp.zeros((Q_SUB, LANES), F32)
            for h in range(HEADS_PER_GROUP):
                qc = rr * GROUP_QKV + h * HEAD_DIM
                kc = qc + GROUP_WIDTH
                c = (rr * HEADS_PER_GROUP + h) * 2 * HEAD_DIM
                sc = lax.dot_general(qkv_ref[0, rows, qc:qc + HEAD_DIM], qkv_ref[0, keys, kc:kc + HEAD_DIM],
                                     (((1,), (1,)), ((), ())), preferred_element_type=F32)
                sc = sc + bias_ref[variant, h]
                m = jnp.max(sc, axis=-1, keepdims=True)
                p = jnp.exp(sc - m)
                pv = jnp.dot(p.astype(BF16), vaug_ref[keys, c:c + 2 * HEAD_DIM], preferred_element_type=F32)
                den = pv[:, HEAD_DIM:]
                oc = rr * GROUP_WIDTH + h * HEAD_DIM
                o_ref[0, rows, oc:oc + HEAD_DIM] = (pv[:, :HEAD_DIM] / den).astype(BF16)
                lse_tile = jnp.where(lane == h, m + jnp.log(den), lse_tile)
            lse_ref[0, rows, rr * LANES:(rr + 1) * LANES] = lse_tile


def _attention_group(qkv, bias, group, batch, seq_full):
    dilation = ATTN_GROUPS[group][1]
    seq = seq_full // dilation
    whole = lambda b: (b, 0, 0)
    o, lse = pl.pallas_call(
        functools.partial(_attn_kernel, seq=seq, res=dilation),
        grid=(batch,),
        in_specs=[pl.BlockSpec((1, seq, dilation * GROUP_QKV), whole),
                  _resident((3, HEADS_PER_GROUP, Q_SUB, K_WIN), lambda b: (0, 0, 0, 0))],
        out_specs=[pl.BlockSpec((1, seq, dilation * GROUP_WIDTH), whole),
                   pl.BlockSpec((1, seq, dilation * LANES), whole)],
        out_shape=[jax.ShapeDtypeStruct((batch, seq, dilation * GROUP_WIDTH), BF16),
                   jax.ShapeDtypeStruct((batch, seq, dilation * LANES), F32)],
        scratch_shapes=[pltpu.VMEM((seq, dilation * 2 * GROUP_WIDTH), BF16)],
        compiler_params=_cparams("parallel"),
        name=f"attn_g{group}",
    )(qkv.reshape(batch, seq, dilation * GROUP_QKV), bias)
    return o.reshape(batch * seq, dilation * GROUP_WIDTH), lse.reshape(batch * seq, dilation * LANES)


def _dft_tables(seq):
    c = jnp.arange(HEAD_DIM, dtype=jnp.int32)
    ang_c = ((c[:, None] * c[None, :]) % HEAD_DIM).astype(F32) * (2.0 * math.pi / HEAD_DIM)
    chan = jnp.concatenate([jnp.cos(ang_c), jnp.sin(ang_c)], axis=1) * (HEAD_DIM ** -0.5)
    tiles = seq // 2 // DFT_TILE
    s = jnp.arange(seq, dtype=jnp.int32)

    def angle(k):
        return ((k[:, None] * s[None, :]) % seq).astype(F32) * (2.0 * math.pi / seq)

    ang_j = angle(jnp.arange(DFT_TILE + DFT_EXTRA, dtype=jnp.int32))
    ang_m = angle(jnp.arange(tiles, dtype=jnp.int32) * DFT_TILE)
    cj, sj = jnp.cos(ang_j)[None], jnp.sin(ang_j)[None]
    cm, sm = jnp.cos(ang_m)[:, None, :], jnp.sin(ang_m)[:, None, :]
    cos_t = (cm * cj - sm * sj) * (seq ** -0.5)
    sin_t = (sm * cj + cm * sj) * (seq ** -0.5)
    j = np.arange(DFT_TILE)[:, None]
    flip = (np.arange(DFT_TILE + DFT_EXTRA)[None, :] == DFT_TILE - j).astype(np.float32)
    return chan.astype(BF16), cos_t.astype(BF16), sin_t.astype(BF16), jnp.asarray(flip, BF16)


def _fourier_kernel(u_ref, chan_ref, cos_ref, sin_ref, flip_ref, o_ref, y_ref, *, seq):
    m = pl.program_id(1)

    @pl.when(m == 0)
    def _():
        rows = 512
        for r0 in range(0, seq, rows):
            for g in range(FOURIER_GROUPS):
                y = jnp.dot(u_ref[0, r0:r0 + rows, g * HEAD_DIM:(g + 1) * HEAD_DIM], chan_ref[...],
                            preferred_element_type=F32)
                y_ref[r0:r0 + rows, g * HEAD_DIM:(g + 1) * HEAD_DIM] = y[:, :HEAD_DIM].astype(BF16)
                y_ref[seq + r0:seq + r0 + rows, g * HEAD_DIM:(g + 1) * HEAD_DIM] = y[:, HEAD_DIM:].astype(BF16)

    even = jnp.dot(cos_ref[0], y_ref[0:seq, :], preferred_element_type=F32)
    odd = jnp.dot(sin_ref[0], y_ref[seq:2 * seq, :], preferred_element_type=F32)
    lo = pl.multiple_of(m * DFT_TILE, DFT_TILE)
    o_ref[0, pl.ds(lo, DFT_TILE), :] = (even - odd)[:DFT_TILE].astype(BF16)
    mirrored = jnp.dot(flip_ref[...], (even + odd).astype(BF16), preferred_element_type=F32)
    hi = pl.multiple_of(seq - lo - DFT_TILE, DFT_TILE)
    o_ref[0, pl.ds(hi, DFT_TILE), :] = mirrored.astype(BF16)


def _fourier(uf, tables, batch, seq):
    chan, cos_t, sin_t, flip = tables
    u3 = uf.reshape(batch, seq, FOURIER_WIDTH)
    rows = DFT_TILE + DFT_EXTRA
    out = pl.pallas_call(
        functools.partial(_fourier_kernel, seq=seq),
        grid=(batch, seq // 2 // DFT_TILE),
        in_specs=[pl.BlockSpec((1, seq, FOURIER_WIDTH), lambda b, m: (b, 0, 0)),
                  pl.BlockSpec((HEAD_DIM, 2 * HEAD_DIM), lambda b, m: (0, 0)),
                  pl.BlockSpec((1, rows, seq), lambda b, m: (m, 0, 0)),
                  pl.BlockSpec((1, rows, seq), lambda b, m: (m, 0, 0)),
                  pl.BlockSpec((DFT_TILE, rows), lambda b, m: (0, 0))],
        out_specs=pl.BlockSpec((1, seq, FOURIER_WIDTH), lambda b, m: (b, 0, 0)),
        out_shape=jax.ShapeDtypeStruct((batch, seq, FOURIER_WIDTH), BF16),
        scratch_shapes=[pltpu.VMEM((2 * seq, FOURIER_WIDTH), BF16)],
        compiler_params=_cparams("parallel", "arbitrary"),
        name="fourier",
    )(u3, chan, cos_t, sin_t, flip)
    return out.reshape(batch * seq, FOURIER_WIDTH)


def _pool_kernel(u_ref, w_ref, scale_ref, o_ref, pad_ref, *, seq):
    rows = 512
    zeros = jnp.zeros((POOL_PAD, HEAD_DIM), F32)
    pad_ref[0:POOL_PAD, :] = zeros
    pad_ref[POOL_PAD + seq:2 * POOL_PAD + seq, :] = zeros
    for g, window in enumerate(POOL_WINDOWS):
        cols = slice(g * HEAD_DIM, (g + 1) * HEAD_DIM)
        half = window // 2
        pad_ref[POOL_PAD:POOL_PAD + seq, :] = u_ref[0, :, cols].astype(F32)
        for r0 in range(0, seq, rows):
            base = POOL_PAD + r0
            acc = pad_ref[base - half:base - half + rows, :]
            for off in range(-half + 1, half):
                acc = acc + pad_ref[base + off:base + off + rows, :]
            pos = r0 + lax.broadcasted_iota(jnp.int32, (rows, 1), 0)
            cnt = jnp.minimum(pos + half - 1, seq - 1) - jnp.maximum(pos - half, 0) + 1
            d = acc / cnt.astype(F32) - pad_ref[base:base + rows, :]
            y = jnp.dot(d.astype(BF16), w_ref[g], preferred_element_type=F32) * scale_ref[:, cols]
            o_ref[0, r0:r0 + rows, cols] = y.astype(BF16)


def _pool(up, w, layer, scale, batch, seq):
    u3 = up.reshape(batch, seq, POOL_WIDTH)
    out = pl.pallas_call(
        functools.partial(_pool_kernel, seq=seq),
        grid=(batch,),
        in_specs=[pl.BlockSpec((1, seq, POOL_WIDTH), lambda b: (b, 0, 0)),
                  _layer_block((len(POOL_WINDOWS), HEAD_DIM, HEAD_DIM), layer),
                  pl.BlockSpec((1, POOL_WIDTH), lambda b: (0, 0))],
        out_specs=pl.BlockSpec((1, seq, POOL_WIDTH), lambda b: (b, 0, 0)),
        out_shape=jax.ShapeDtypeStruct((batch, seq, POOL_WIDTH), BF16),
        scratch_shapes=[pltpu.VMEM((seq + 2 * POOL_PAD, HEAD_DIM), F32)],
        compiler_params=_cparams("parallel"),
        name="pool",
    )(u3, w, scale)
    return out.reshape(batch * seq, POOL_WIDTH)


def _merge_kernel(x_ref, o1_ref, o2_ref, o3_ref, l1_ref, l2_ref, l3_ref, four_ref, pool_ref, gate_ref,
                  wba_ref, wbf_ref, wbp_ref, wout_ref, xo_ref, *scratch):
    n_dilated = len(ATTN_GROUPS) - 1
    oslabs = scratch[:n_dilated * HEADS_PER_GROUP]
    lslabs = scratch[n_dilated * HEADS_PER_GROUP:-2]
    attn_ref, acc_ref = scratch[-2:]
    tm = x_ref.shape[0]
    o_refs = (o1_ref, o2_ref, o3_ref)
    l_refs = (l1_ref, l2_ref, l3_ref)
    f = jnp.dot(four_ref[...], wbf_ref[...], preferred_element_type=F32)
    p = jnp.dot(pool_ref[...], wbp_ref[...], preferred_element_type=F32)
    acc_ref[...] = (gate_ref[:, D_MODEL:2 * D_MODEL].astype(F32) * f
                    + gate_ref[:, 2 * D_MODEL:3 * D_MODEL].astype(F32) * p)
    dilated = [(g, d) for g, (_, d) in enumerate(ATTN_GROUPS) if d > 1]
    for g, d in dilated:
        for r in range(d):
            lslabs[g - 1][pl.ds(r, tm // d, stride=d), :] = l_refs[g][:, r * LANES:(r + 1) * LANES]
    lses = [l1_ref[...]] + [ref[...] for ref in lslabs]
    top = functools.reduce(jnp.maximum, lses)
    es = [jnp.exp(l - top) for l in lses]
    inv = 1.0 / functools.reduce(lambda a, b: a + b, es)
    weights = [e * inv for e in es]
    for j in range(HEADS_PER_GROUP):
        cols = slice(j * HEAD_DIM, (j + 1) * HEAD_DIM)
        for g, d in dilated:
            slab = oslabs[(g - 1) * HEADS_PER_GROUP + j]
            for r in range(d):
                lo = r * GROUP_WIDTH + j * HEAD_DIM
                slab[pl.ds(r, tm // d, stride=d), :] = o_refs[g][:, lo:lo + HEAD_DIM].astype(F32)
        outs = [o1_ref[:, cols].astype(F32)] + [oslabs[(g - 1) * HEADS_PER_GROUP + j][...] for g, _ in dilated]
        attn_ref[:, cols] = sum(w[:, j:j + 1] * o for w, o in zip(weights, outs)).astype(BF16)
    a = jnp.dot(attn_ref[...], wba_ref[...], preferred_element_type=F32)
    merged = gate_ref[:, 0:D_MODEL].astype(F32) * a + acc_ref[...]
    xo_ref[...] = x_ref[...] + jnp.dot(merged.astype(BF16), wout_ref[...], preferred_element_type=F32)


def _merge(x2, os_, ls_, four, pool, gates, wba, wbf, wbp, wout, layer, tm=512):
    n = x2.shape[0]
    row = lambda i: (i, 0)
    return pl.pallas_call(
        _merge_kernel,
        grid=(n // tm,),
        in_specs=[pl.BlockSpec((tm, D_MODEL), row)]
        + [pl.BlockSpec((tm // d, d * GROUP_WIDTH), row) for _, d in ATTN_GROUPS]
        + [pl.BlockSpec((tm // d, d * LANES), row) for _, d in ATTN_GROUPS]
        + [pl.BlockSpec((tm, FOURIER_WIDTH), row), pl.BlockSpec((tm, POOL_WIDTH), row),
           pl.BlockSpec((tm, GATE_WIDTH), row)]
        + [_layer_block((GROUP_WIDTH, D_MODEL), layer)] * 3
        + [_layer_block((D_MODEL, D_MODEL), layer)],
        out_specs=pl.BlockSpec((tm, D_MODEL), row),
        out_shape=jax.ShapeDtypeStruct((n, D_MODEL), F32),
        scratch_shapes=[pltpu.VMEM((tm, LANES), F32)] * ((len(ATTN_GROUPS) - 1) * (HEADS_PER_GROUP + 1))
        + [pltpu.VMEM((tm, GROUP_WIDTH), BF16), pltpu.VMEM((tm, D_MODEL), F32)],
        compiler_params=_cparams("parallel"),
        name="merge",
    )(x2, *os_, *ls_, four, pool, gates, wba, wbf, wbp, wout)


FFN_CHUNK = 256


def _silu(a):
    return a * (0.5 * jnp.tanh(0.5 * a) + 0.5)


def _ffn_kernel(x_ref, gain_ref, w1_ref, w3_ref, w2_ref, o_ref, h_ref, g_ref):
    h_ref[...] = _rms(x_ref[...], gain_ref[...]).astype(BF16)
    ffn = w1_ref.shape[1]
    for c0 in range(0, ffn, FFN_CHUNK):
        a = jnp.dot(h_ref[...], w1_ref[:, c0:c0 + FFN_CHUNK], preferred_element_type=F32)
        b = jnp.dot(h_ref[...], w3_ref[:, c0:c0 + FFN_CHUNK], preferred_element_type=F32)
        g_ref[:, c0:c0 + FFN_CHUNK] = (_silu(a) * b).astype(BF16)
    o_ref[...] = x_ref[...] + jnp.dot(g_ref[...], w2_ref[...], preferred_element_type=F32)


def _dense_ffn(x2, gain, w1, w3, w2, layer, tm=1024):
    n = x2.shape[0]
    ffn = w1.shape[2]
    row = lambda i: (i, 0)
    return pl.pallas_call(
        _ffn_kernel,
        grid=(n // tm,),
        in_specs=[pl.BlockSpec((tm, D_MODEL), row), pl.BlockSpec((1, D_MODEL), lambda i: (0, 0)),
                  _layer_block((D_MODEL, ffn), layer, resident=True),
                  _layer_block((D_MODEL, ffn), layer, resident=True),
                  _layer_block((ffn, D_MODEL), layer, resident=True)],
        out_specs=pl.BlockSpec((tm, D_MODEL), row),
        out_shape=jax.ShapeDtypeStruct((n, D_MODEL), F32),
        scratch_shapes=[pltpu.VMEM((tm, D_MODEL), BF16), pltpu.VMEM((tm, ffn), BF16)],
        compiler_params=_cparams("parallel"),
        name="dense_ffn",
    )(x2, gain, w1, w3, w2)


R_E0, R_E1, R_G0, R_G1, R_RANK0, R_RANK1 = 0, 1, 2, 3, 4, 5


def _router_kernel(x_ref, g_ref, rw_ref, rb_ref, meta_ref, metat_ref, cnt_ref, *, tm):
    h = _rms(x_ref[...], g_ref[...])
    logits = jnp.dot(h.astype(BF16), rw_ref[...], preferred_element_type=F32) + rb_ref[...]
    lane = lax.broadcasted_iota(jnp.int32, (tm, LANES), 1)
    logits = jnp.where(lane < N_EXPERTS, logits, -jnp.inf)
    v0 = jnp.max(logits, axis=-1, keepdims=True)
    e0 = jnp.min(jnp.where(logits == v0, lane, LANES), axis=-1, keepdims=True)
    rest = jnp.where(lane == e0, -jnp.inf, logits)
    v1 = jnp.max(rest, axis=-1, keepdims=True)
    e1 = jnp.min(jnp.where(rest == v1, lane, LANES), axis=-1, keepdims=True)
    t = jnp.exp(v1 - v0)
    gate0 = 1.0 / (1.0 + t)
    gate1 = t / (1.0 + t)

    picked = jnp.logical_or(lane == e0, lane == e1)
    tri = (lax.broadcasted_iota(jnp.int32, (tm, tm), 1) < lax.broadcasted_iota(jnp.int32, (tm, tm), 0))
    before = jnp.dot(tri.astype(BF16), picked.astype(BF16), preferred_element_type=F32)
    rank0 = jnp.sum(jnp.where(lane == e0, before, 0.0), axis=-1, keepdims=True)
    rank1 = jnp.sum(jnp.where(lane == e1, before, 0.0), axis=-1, keepdims=True)

    meta = jnp.zeros((tm, LANES), F32)
    for ln, val in ((R_E0, e0.astype(F32)), (R_E1, e1.astype(F32)), (R_G0, gate0), (R_G1, gate1),
                    (R_RANK0, rank0), (R_RANK1, rank1)):
        meta = jnp.where(lane == ln, val, meta)
    meta_ref[...] = meta
    metat_ref[0] = jnp.transpose(meta)[:8]
    cnt_ref[0] = jnp.sum(picked.astype(F32), axis=0, keepdims=True)


def _router(x2, g, rw, rb, tm):
    n = x2.shape[0]
    rw_p = jnp.zeros((D_MODEL, LANES), BF16).at[:, :N_EXPERTS].set(rw.astype(BF16))
    rb_p = jnp.zeros((1, LANES), F32).at[0, :N_EXPERTS].set(rb.astype(F32))
    return pl.pallas_call(
        functools.partial(_router_kernel, tm=tm),
        grid=(n // tm,),
        in_specs=[pl.BlockSpec((tm, D_MODEL), lambda i: (i, 0)),
                  pl.BlockSpec((1, D_MODEL), lambda i: (0, 0)),
                  pl.BlockSpec((D_MODEL, LANES), lambda i: (0, 0)),
                  pl.BlockSpec((1, LANES), lambda i: (0, 0))],
        out_specs=[pl.BlockSpec((tm, LANES), lambda i: (i, 0)),
                   pl.BlockSpec((1, 8, tm), lambda i: (i, 0, 0)),
                   pl.BlockSpec((1, 1, LANES), lambda i: (i, 0, 0))],
        out_shape=[jax.ShapeDtypeStruct((n, LANES), F32),
                   jax.ShapeDtypeStruct((n // tm, 8, tm), F32),
                   jax.ShapeDtypeStruct((n // tm, 1, LANES), F32)],
        compiler_params=_cparams("parallel"),
        name="router",
    )(x2, g, rw_p, rb_p)


MOE_TILE = 512
RUN_ALIGN = 16
RUN_CHUNKS = (512, 256, 128, 64, 32, 16)
SORT_ROWS = -(-(MOE_TILE * TOP_K + N_EXPERTS * (RUN_ALIGN - 1)) // LANES) * LANES


def _run_copies(i, loc_ref, glob_ref, len_ref, make_copy, action):
    for e in range(N_EXPERTS):
        n = len_ref[i * N_EXPERTS + e]
        lo = loc_ref[i * N_EXPERTS + e]
        go = glob_ref[i * N_EXPERTS + e]
        for size in RUN_CHUNKS:
            done = n & ~(2 * size - 1)

            @pl.when((n & size) != 0)
            def _():
                action(make_copy(pl.multiple_of(lo + done, RUN_ALIGN), pl.multiple_of(go + done, RUN_ALIGN), size))


def _run_base(i, loc_ref, expert):
    base = jnp.zeros_like(expert)
    for e in range(N_EXPERTS):
        base = jnp.where(expert == e, loc_ref[i * N_EXPERTS + e], base)
    return base


def _dispatch_kernel(loc_ref, glob_ref, len_ref, x_ref, g_ref, metat_ref, xs_hbm, sorted_ref, zero_ref, sem):
    i = pl.program_id(0)
    n_tiles = pl.num_programs(0)

    @pl.when(i == 0)
    def _():
        zero_ref[...] = jnp.zeros_like(zero_ref)

        def make_copy(lo, go, size):
            return pltpu.make_async_copy(zero_ref.at[pl.ds(lo, size), :], xs_hbm.at[pl.ds(go, size), :], sem.at[2])

        _run_copies(n_tiles, loc_ref, glob_ref, len_ref, make_copy, lambda c: c.start())
        _run_copies(n_tiles, loc_ref, glob_ref, len_ref, make_copy, lambda c: c.wait())
        last = n_tiles * N_EXPERTS + N_EXPERTS - 1
        first_unused = (glob_ref[last] + len_ref[last]) // EXPERT_BLOCK

        def clear(j, carry):
            block = make_copy(0, pl.multiple_of(j * EXPERT_BLOCK, EXPERT_BLOCK), EXPERT_BLOCK)
            block.start()
            block.wait()
            return carry

        lax.fori_loop(first_unused, xs_hbm.shape[0] // EXPERT_BLOCK, clear, 0)

    h = _rms(x_ref[...], g_ref[...]).astype(BF16)
    rec = metat_ref[0]
    slot0 = _run_base(i, loc_ref, rec[R_E0:R_E0 + 1].astype(jnp.int32)) + rec[R_RANK0:R_RANK0 + 1].astype(jnp.int32)
    slot1 = _run_base(i, loc_ref, rec[R_E1:R_E1 + 1].astype(jnp.int32)) + rec[R_RANK1:R_RANK1 + 1].astype(jnp.int32)
    rows = lax.broadcasted_iota(jnp.int32, (SORT_ROWS, MOE_TILE), 0)
    perm = jnp.logical_or(rows == slot0, rows == slot1).astype(BF16)
    sorted_ref[i % 2] = jnp.dot(perm, h, preferred_element_type=F32).astype(BF16)

    def copies(tile, action):
        def make_copy(lo, go, size):
            return pltpu.make_async_copy(sorted_ref.at[tile % 2, pl.ds(lo, size), :],
                                         xs_hbm.at[pl.ds(go, size), :], sem.at[tile % 2])
        _run_copies(tile, loc_ref, glob_ref, len_ref, make_copy, action)

    copies(i, lambda c: c.start())

    @pl.when(i > 0)
    def _():
        copies(i - 1, lambda c: c.wait())

    @pl.when(i == n_tiles - 1)
    def _():
        copies(i, lambda c: c.wait())


def _dispatch(x2, g, metat, loc, glob, plen, total):
    n = x2.shape[0]
    return pl.pallas_call(
        _dispatch_kernel,
        grid_spec=pltpu.PrefetchScalarGridSpec(
            num_scalar_prefetch=3,
            grid=(n // MOE_TILE,),
            in_specs=[pl.BlockSpec((MOE_TILE, D_MODEL), lambda i, *_: (i, 0)),
                      pl.BlockSpec((1, D_MODEL), lambda i, *_: (0, 0)),
                      pl.BlockSpec((1, 8, MOE_TILE), lambda i, *_: (i, 0, 0))],
            out_specs=pl.BlockSpec(memory_space=pl.ANY),
            scratch_shapes=[pltpu.VMEM((2, SORT_ROWS, D_MODEL), BF16), pltpu.VMEM((EXPERT_BLOCK, D_MODEL), BF16),
                            pltpu.SemaphoreType.DMA((3,))]),
        out_shape=jax.ShapeDtypeStruct((total, D_MODEL), BF16),
        compiler_params=_cparams("arbitrary"),
        name="moe_dispatch",
    )(loc, glob, plen, x2, g, metat)


MOE_CHUNK = 512


def _expert_kernel(blk_e_ref, n_used_ref, xs_ref, w1_ref, w3_ref, w2_ref, ys_ref, act_ref):
    j = pl.program_id(0)

    @pl.when(j < n_used_ref[0])
    def _():
        h = xs_ref[...]
        ffn = w1_ref.shape[2]
        for c0 in range(0, ffn, MOE_CHUNK):
            a = jnp.dot(h, w1_ref[0, :, c0:c0 + MOE_CHUNK], preferred_element_type=F32)
            b = jnp.dot(h, w3_ref[0, :, c0:c0 + MOE_CHUNK], preferred_element_type=F32)
            act_ref[:, c0:c0 + MOE_CHUNK] = (_silu(a) * b).astype(BF16)
        ys_ref[...] = jnp.dot(act_ref[...], w2_ref[0], preferred_element_type=F32).astype(BF16)

    @pl.when(j >= n_used_ref[0])
    def _():
        ys_ref[...] = jnp.zeros_like(ys_ref)


def _experts(xs, w1, w3, w2, layer, blk_e, n_used):
    total = xs.shape[0]
    ffn = w1.shape[3]
    n_blocks = total // EXPERT_BLOCK
    expert = lambda j, be, nu: (layer, be[j], 0, 0)
    return pl.pallas_call(
        _expert_kernel,
        grid_spec=pltpu.PrefetchScalarGridSpec(
            num_scalar_prefetch=2,
            grid=(n_blocks,),
            in_specs=[pl.BlockSpec((EXPERT_BLOCK, D_MODEL), lambda j, be, nu: (jnp.minimum(j, nu[0] - 1), 0)),
                      pl.BlockSpec((None, 1, D_MODEL, ffn), expert),
                      pl.BlockSpec((None, 1, D_MODEL, ffn), expert),
                      _resident((None, 1, ffn, D_MODEL), expert)],
            out_specs=pl.BlockSpec((EXPERT_BLOCK, D_MODEL), lambda j, be, nu: (j, 0)),
            scratch_shapes=[pltpu.VMEM((EXPERT_BLOCK, ffn), BF16)]),
        out_shape=jax.ShapeDtypeStruct((total, D_MODEL), BF16),
        compiler_params=_cparams("arbitrary"),
        name="moe_experts",
    )(blk_e, n_used, xs, w1, w3, w2)


def _combine_kernel(loc_ref, glob_ref, len_ref, x_ref, meta_ref, ys_hbm, o_ref, buf_ref, sem):
    i = pl.program_id(0)

    def copies(tile, action):
        def make_copy(lo, go, size):
            return pltpu.make_async_copy(ys_hbm.at[pl.ds(go, size), :],
                                         buf_ref.at[tile % 2, pl.ds(lo, size), :], sem.at[tile % 2])
        _run_copies(tile, loc_ref, glob_ref, len_ref, make_copy, action)

    @pl.when(i == 0)
    def _():
        copies(i, lambda c: c.start())

    @pl.when(i + 1 < pl.num_programs(0))
    def _():
        copies(i + 1, lambda c: c.start())

    rec = meta_ref[...]
    slot0 = _run_base(i, loc_ref, rec[:, R_E0:R_E0 + 1].astype(jnp.int32)) + rec[:, R_RANK0:R_RANK0 + 1].astype(jnp.int32)
    slot1 = _run_base(i, loc_ref, rec[:, R_E1:R_E1 + 1].astype(jnp.int32)) + rec[:, R_RANK1:R_RANK1 + 1].astype(jnp.int32)
    cols = lax.broadcasted_iota(jnp.int32, (MOE_TILE, SORT_ROWS), 1)
    weight = jnp.where(cols == slot0, rec[:, R_G0:R_G0 + 1],
                       jnp.where(cols == slot1, rec[:, R_G1:R_G1 + 1], 0.0)).astype(BF16)
    copies(i, lambda c: c.wait())
    filled = loc_ref[i * N_EXPERTS + N_EXPERTS - 1] + len_ref[i * N_EXPERTS + N_EXPERTS - 1]
    rows = lax.broadcasted_iota(jnp.int32, (SORT_ROWS, 1), 0)
    picked = jnp.where(rows < filled, buf_ref[i % 2], jnp.zeros((), BF16))
    o_ref[...] = x_ref[...] + jnp.dot(weight, picked, preferred_element_type=F32)


def _combine(x2, meta, ys, loc, glob, plen):
    n = x2.shape[0]
    return pl.pallas_call(
        _combine_kernel,
        grid_spec=pltpu.PrefetchScalarGridSpec(
            num_scalar_prefetch=3,
            grid=(n // MOE_TILE,),
            in_specs=[pl.BlockSpec((MOE_TILE, D_MODEL), lambda i, *_: (i, 0)),
                      pl.BlockSpec((MOE_TILE, LANES), lambda i, *_: (i, 0)),
                      pl.BlockSpec(memory_space=pl.ANY)],
            out_specs=pl.BlockSpec((MOE_TILE, D_MODEL), lambda i, *_: (i, 0)),
            scratch_shapes=[pltpu.VMEM((2, SORT_ROWS, D_MODEL), BF16), pltpu.SemaphoreType.DMA((2,))]),
        out_shape=jax.ShapeDtypeStruct((n, D_MODEL), F32),
        compiler_params=_cparams("arbitrary"),
        name="moe_combine",
    )(loc, glob, plen, x2, meta, ys)


def _moe(x2, g, rw, rb, w1, w3, w2, layer):
    n = x2.shape[0]
    n_tiles = n // MOE_TILE
    meta, metat, cnt = _router(x2, g, rw, rb, MOE_TILE)
    cnt = cnt[:, 0, :N_EXPERTS].astype(jnp.int32)
    plen = (cnt + RUN_ALIGN - 1) // RUN_ALIGN * RUN_ALIGN
    loc = jnp.cumsum(plen, axis=1) - plen
    filled = jnp.sum(plen, axis=0)
    padded = (filled + EXPERT_BLOCK - 1) // EXPERT_BLOCK * EXPERT_BLOCK
    pend = jnp.cumsum(padded)
    glob = (pend - padded)[None, :] + jnp.cumsum(plen, axis=0) - plen
    loc = jnp.concatenate([loc, jnp.zeros((1, N_EXPERTS), jnp.int32)])
    glob = jnp.concatenate([glob, (pend - padded + filled)[None, :]])
    plen = jnp.concatenate([plen, (padded - filled)[None, :]])
    n_blocks = -(-(n * TOP_K + n_tiles * N_EXPERTS * (RUN_ALIGN - 1)) // EXPERT_BLOCK) + N_EXPERTS
    blk_start = jnp.arange(n_blocks, dtype=jnp.int32) * EXPERT_BLOCK
    blk_e = jnp.minimum(jnp.sum(blk_start[:, None] >= pend[None, :], axis=1), N_EXPERTS - 1).astype(jnp.int32)
    n_used = (pend[-1] // EXPERT_BLOCK).astype(jnp.int32).reshape(1)
    loc, glob, plen = (a.reshape(-1).astype(jnp.int32) for a in (loc, glob, plen))
    xs = _dispatch(x2, g, metat, loc, glob, plen, n_blocks * EXPERT_BLOCK)
    ys = _experts(xs, w1, w3, w2, layer, blk_e, n_used)
    return _combine(x2, meta, ys, loc, glob, plen)


def kernel(x, norm_mix_g, w_in, q_norm_g, k_norm_g, rel_bias, pool_w, pool_scale, w_branch_attn,
           w_branch_fourier, w_branch_pool, w_out, norm_ffn_g, ffn_w1, ffn_w3, ffn_w2,
           router_w, router_b, moe_w1, moe_w3, moe_w2):
    batch, seq, _ = x.shape
    depth = w_in.shape[0]
    n = batch * seq
    x2 = x.reshape(n, D_MODEL)
    biases = [_attn_bias(rel_bias, g) for g in range(len(ATTN_GROUPS))]
    dft = _dft_tables(seq)
    row = lambda v: v.reshape(1, -1).astype(F32)
    (w_in, pool_w, w_branch_attn, w_branch_fourier, w_branch_pool, w_out, ffn_w1, ffn_w3, ffn_w2,
     moe_w1, moe_w3, moe_w2) = (w.astype(BF16) for w in (
         w_in, pool_w, w_branch_attn, w_branch_fourier, w_branch_pool, w_out, ffn_w1, ffn_w3, ffn_w2,
         moe_w1, moe_w3, moe_w2))
    for layer in range(depth):
        *qkvs, uf, up, gates = _in_proj(x2, row(norm_mix_g[layer]), w_in, layer,
                                        row(q_norm_g[layer]) * (HEAD_DIM ** -0.5), row(k_norm_g[layer]))
        os_, ls_ = [], []
        for g in range(len(ATTN_GROUPS)):
            o, lse = _attention_group(qkvs[g], biases[g], g, batch, seq)
            os_.append(o)
            ls_.append(lse)
        four = _fourier(uf, dft, batch, seq)
        pool = _pool(up, pool_w, layer, row(pool_scale[layer]), batch, seq)
        j = layer // 2
        x2 = _merge(x2, os_, ls_, four, pool, gates, w_branch_attn, w_branch_fourier, w_branch_pool, w_out, layer)
        if layer % 2 == 0:
            x2 = _dense_ffn(x2, row(norm_ffn_g[layer]), ffn_w1, ffn_w3, ffn_w2, j)
        else:
            x2 = _moe(x2, row(norm_ffn_g[layer]), router_w[j], router_b[j], moe_w1, moe_w3, moe_w2, j)
    return x2.reshape(batch, seq, D_MODEL)
```

```python
import functools
import math

import numpy as np
import jax
import jax.numpy as jnp
from jax import lax
from jax.experimental import pallas as pl
from jax.experimental.pallas import tpu as pltpu

F32 = jnp.float32
BF16 = jnp.bfloat16

D_MODEL = 1024
HEAD_DIM = 128
ATTN_GROUPS = ((128, 1), (512, 4), (2048, 16))
HEADS_PER_GROUP = 4
N_ATTN_HEADS = HEADS_PER_GROUP * len(ATTN_GROUPS)
ATTN_WIDTH = N_ATTN_HEADS * HEAD_DIM
GROUP_WIDTH = HEADS_PER_GROUP * HEAD_DIM
QKV_WIDTH = 3 * ATTN_WIDTH
GROUP_QKV = 3 * GROUP_WIDTH
N_BUCKETS = 32
REL_MAX_DISTANCE = 1024
FOURIER_GROUPS = 4
FOURIER_WIDTH = 512
POOL_WINDOWS = (2, 4, 8, 16)
POOL_WIDTH = 512
N_BRANCHES = 3
GATE_WIDTH = N_BRANCHES * D_MODEL
IN_COLS = QKV_WIDTH + FOURIER_WIDTH + POOL_WIDTH + GATE_WIDTH
N_EXPERTS = 8
TOP_K = 2
EXPERT_BLOCK = 512
RMS_EPS = 1e-6
NEG_INF = -1e30

LANES = 128
HALF_WINDOW = 64
Q_SUB = 128
K_WIN = Q_SUB + 2 * HALF_WINDOW
POOL_PAD = 8
DFT_TILE = 512
DFT_EXTRA = 8
VMEM_LIMIT = 56 * 1024 * 1024


def _cparams(*sem):
    return pltpu.CompilerParams(dimension_semantics=sem, vmem_limit_bytes=VMEM_LIMIT)


def _resident(shape, index_map):
    return pl.BlockSpec(shape, index_map, pipeline_mode=pl.Buffered(1))


def _layer_block(shape, layer, resident=False):
    index_map = lambda *_: (layer,) + (0,) * len(shape)
    make = _resident if resident else pl.BlockSpec
    return make((None,) + tuple(shape), index_map)


def _rms(x, gain):
    return x * lax.rsqrt(jnp.mean(x * x, axis=-1, keepdims=True) + RMS_EPS) * gain


IN_CHUNK = 512
EPI_ROWS = 128


def _in_proj_kernel(x_ref, g_ref, w_ref, qg_ref, kg_ref, qkv0_ref, qkv1_ref, qkv2_ref, uf_ref, up_ref, gate_ref,
                    h_ref, *slab_refs):
    tm = x_ref.shape[0]
    qkv_refs = (qkv0_ref, qkv1_ref, qkv2_ref)
    h_ref[...] = _rms(x_ref[...], g_ref[...]).astype(BF16)
    pieces = IN_CHUNK // LANES
    for c in range(IN_COLS // IN_CHUNK):
        col = c * IN_CHUNK
        acc = jnp.dot(h_ref[...], w_ref[:, col:col + IN_CHUNK], preferred_element_type=F32)
        slabs = slab_refs[(c % 2) * pieces:(c % 2 + 1) * pieces]
        for hh in range(pieces):
            slabs[hh][...] = acc[:, hh * LANES:(hh + 1) * LANES]
        if col < QKV_WIDTH:
            kind, group = divmod(c, len(ATTN_GROUPS))
            dilation = ATTN_GROUPS[group][1]
            out_ref = qkv_refs[group]
            for hh in range(HEADS_PER_GROUP):
                lo = kind * GROUP_WIDTH + hh * HEAD_DIM
                for r in range(dilation):
                    for r0 in range(0, tm // dilation, EPI_ROWS):
                        n_rows = min(EPI_ROWS, tm // dilation)
                        if dilation == 1:
                            a = slabs[hh][r0:r0 + n_rows, :]
                        else:
                            a = slabs[hh][pl.ds(r0 * dilation + r, n_rows, stride=dilation), :]
                        if kind < 2:
                            a = _rms(a, qg_ref[...] if kind == 0 else kg_ref[...])
                        c0 = r * GROUP_QKV + lo
                        out_ref[r0:r0 + n_rows, c0:c0 + HEAD_DIM] = a.astype(BF16)
        elif col < QKV_WIDTH + FOURIER_WIDTH + POOL_WIDTH:
            out_ref = uf_ref if col < QKV_WIDTH + FOURIER_WIDTH else up_ref
            for hh in range(pieces):
                for r0 in range(0, tm, EPI_ROWS):
                    out_ref[r0:r0 + EPI_ROWS, hh * LANES:(hh + 1) * LANES] = slabs[hh][r0:r0 + EPI_ROWS, :].astype(BF16)
        else:
            lo = col - (IN_COLS - GATE_WIDTH)
            for hh in range(pieces):
                for r0 in range(0, tm, EPI_ROWS):
                    logit = slabs[hh][r0:r0 + EPI_ROWS, :]
                    gate_ref[r0:r0 + EPI_ROWS, lo + hh * LANES:lo + (hh + 1) * LANES] = logit.astype(BF16)


def _in_proj(x2, g, w, layer, qg, kg, tm=512):
    n = x2.shape[0]
    row = lambda i: (i, 0)
    fix = lambda i: (0, 0)
    return pl.pallas_call(
        _in_proj_kernel,
        grid=(n // tm,),
        in_specs=[pl.BlockSpec((tm, D_MODEL), row),
                  pl.BlockSpec((1, D_MODEL), fix),
                  _layer_block((D_MODEL, IN_COLS), layer, resident=True),
                  pl.BlockSpec((1, HEAD_DIM), fix),
                  pl.BlockSpec((1, HEAD_DIM), fix)],
        out_specs=[pl.BlockSpec((tm // d, d * GROUP_QKV), row) for _, d in ATTN_GROUPS]
        + [pl.BlockSpec((tm, FOURIER_WIDTH), row),
           pl.BlockSpec((tm, POOL_WIDTH), row),
           pl.BlockSpec((tm, GATE_WIDTH), row)],
        out_shape=[jax.ShapeDtypeStruct((n // d, d * GROUP_QKV), BF16) for _, d in ATTN_GROUPS]
        + [jax.ShapeDtypeStruct((n, FOURIER_WIDTH), BF16),
           jax.ShapeDtypeStruct((n, POOL_WIDTH), BF16),
           jax.ShapeDtypeStruct((n, GATE_WIDTH), BF16)],
        scratch_shapes=[pltpu.VMEM((tm, D_MODEL), BF16)]
        + [pltpu.VMEM((tm, LANES), F32)] * (2 * IN_CHUNK // LANES),
        compiler_params=_cparams("parallel"),
        name="in_proj",
    )(x2, g, w, qg, kg)


def _t5_bucket(rel):
    half = N_BUCKETS // 2
    ret = (rel > 0).astype(np.int64) * half
    n = np.abs(rel)
    max_exact = half // 2
    large = max_exact + (np.log(np.maximum(n, 1) / max_exact) / np.log(REL_MAX_DISTANCE / max_exact)
                         * (half - max_exact)).astype(np.int64)
    large = np.minimum(large, half - 1)
    return (ret + np.where(n < max_exact, n, large)).astype(np.int32)


def _attn_bias(rel_bias, group):
    dilation = ATTN_GROUPS[group][1]
    r = np.arange(Q_SUB)[:, None]
    c = np.arange(K_WIN)[None, :]
    heads = rel_bias[:, group * HEADS_PER_GROUP:(group + 1) * HEADS_PER_GROUP].astype(F32)
    off = np.stack([c - variant * HALF_WINDOW - r for variant in range(3)])
    valid = np.abs(off) <= HALF_WINDOW
    bucket = _t5_bucket(np.clip(off, -HALF_WINDOW, HALF_WINDOW) * dilation)
    onehot = (jnp.asarray(bucket)[..., None] == jnp.arange(N_BUCKETS)).astype(F32)
    b = jnp.einsum("vqkn,nh->vhqk", onehot, heads, precision=lax.Precision.HIGHEST)
    return jnp.where(valid[:, None], b, NEG_INF)


def _attn_kernel(qkv_ref, bias_ref, o_ref, lse_ref, vaug_ref, *, seq, res):
    for rr in range(res):
        for h in range(HEADS_PER_GROUP):
            c = (rr * HEADS_PER_GROUP + h) * 2 * HEAD_DIM
            vc = rr * GROUP_QKV + 2 * GROUP_WIDTH + h * HEAD_DIM
            vaug_ref[:, c:c + HEAD_DIM] = qkv_ref[0, :, vc:vc + HEAD_DIM]
            vaug_ref[:, c + HEAD_DIM:c + 2 * HEAD_DIM] = jnp.ones((seq, HEAD_DIM), BF16)

    lane = lax.broadcasted_iota(jnp.int32, (Q_SUB, LANES), 1)
    for rr in range(res):
        for s in range(seq // Q_SUB):
            rows = slice(s * Q_SUB, (s + 1) * Q_SUB)
            start = min(max(s * Q_SUB - HALF_WINDOW, 0), seq - K_WIN)
            variant = (s * Q_SUB - start) // HALF_WINDOW
            keys = slice(start, start + K_WIN)
            lse_tile = jnp.zeros((Q_SUB, LANES), F32)
            for h in range(HEADS_PER_GROUP):
                qc = rr * GROUP_QKV + h * HEAD_DIM
                kc = qc + GROUP_WIDTH
                c = (rr * HEADS_PER_GROUP + h) * 2 * HEAD_DIM
                sc = lax.dot_general(qkv_ref[0, rows, qc:qc + HEAD_DIM], qkv_ref[0, keys, kc:kc + HEAD_DIM],
                                     (((1,), (1,)), ((), ())), preferred_element_type=F32)
                sc = sc + bias_ref[variant, h]
                m = jnp.max(sc, axis=-1, keepdims=True)
                p = jnp.exp(sc - m)
                pv = jnp.dot(p.astype(BF16), vaug_ref[keys, c:c + 2 * HEAD_DIM], preferred_element_type=F32)
                den = pv[:, HEAD_DIM:]
                oc = rr * GROUP_WIDTH + h * HEAD_DIM
                o_ref[0, rows, oc:oc + HEAD_DIM] = (pv[:, :HEAD_DIM] / den).astype(BF16)
                lse_tile = jnp.where(lane == h, m + jnp.log(den), lse_tile)
            lse_ref[0, rows, rr * LANES:(rr + 1) * LANES] = lse_tile


def _attention_group(qkv, bias, group, batch, seq_full):
    dilation = ATTN_GROUPS[group][1]
    seq = seq_full // dilation
    whole = lambda b: (b, 0, 0)
    o, lse = pl.pallas_call(
        functools.partial(_attn_kernel, seq=seq, res=dilation),
        grid=(batch,),
        in_specs=[pl.BlockSpec((1, seq, dilation * GROUP_QKV), whole),
                  _resident((3, HEADS_PER_GROUP, Q_SUB, K_WIN), lambda b: (0, 0, 0, 0))],
        out_specs=[pl.BlockSpec((1, seq, dilation * GROUP_WIDTH), whole),
                   pl.BlockSpec((1, seq, dilation * LANES), whole)],
        out_shape=[jax.ShapeDtypeStruct((batch, seq, dilation * GROUP_WIDTH), BF16),
                   jax.ShapeDtypeStruct((batch, seq, dilation * LANES), F32)],
        scratch_shapes=[pltpu.VMEM((seq, dilation * 2 * GROUP_WIDTH), BF16)],
        compiler_params=_cparams("parallel"),
        name=f"attn_g{group}",
    )(qkv.reshape(batch, seq, dilation * GROUP_QKV), bias)
    return o.reshape(batch * seq, dilation * GROUP_WIDTH), lse.reshape(batch * seq, dilation * LANES)


def _dft_tables(seq):
    c = jnp.arange(HEAD_DIM, dtype=jnp.int32)
    ang_c = ((c[:, None] * c[None, :]) % HEAD_DIM).astype(F32) * (2.0 * math.pi / HEAD_DIM)
    chan = jnp.concatenate([jnp.cos(ang_c), jnp.sin(ang_c)], axis=1) * (HEAD_DIM ** -0.5)
    tiles = seq // 2 // DFT_TILE
    s = jnp.arange(seq, dtype=jnp.int32)

    def angle(k):
        return ((k[:, None] * s[None, :]) % seq).astype(F32) * (2.0 * math.pi / seq)

    ang_j = angle(jnp.arange(DFT_TILE + DFT_EXTRA, dtype=jnp.int32))
    ang_m = angle(jnp.arange(tiles, dtype=jnp.int32) * DFT_TILE)
    cj, sj = jnp.cos(ang_j)[None], jnp.sin(ang_j)[None]
    cm, sm = jnp.cos(ang_m)[:, None, :], jnp.sin(ang_m)[:, None, :]
    cos_t = (cm * cj - sm * sj) * (seq ** -0.5)
    sin_t = (sm * cj + cm * sj) * (seq ** -0.5)
    j = np.arange(DFT_TILE)[:, None]
    flip = (np.arange(DFT_TILE + DFT_EXTRA)[None, :] == DFT_TILE - j).astype(np.float32)
    return chan.astype(BF16), cos_t.astype(BF16), sin_t.astype(BF16), jnp.asarray(flip, BF16)


def _fourier_kernel(u_ref, chan_ref, cos_ref, sin_ref, flip_ref, o_ref, y_ref, *, seq):
    m = pl.program_id(1)

    @pl.when(m == 0)
    def _():
        rows = 512
        for r0 in range(0, seq, rows):
            for g in range(FOURIER_GROUPS):
                y = jnp.dot(u_ref[0, r0:r0 + rows, g * HEAD_DIM:(g + 1) * HEAD_DIM], chan_ref[...],
                            preferred_element_type=F32)
                y_ref[r0:r0 + rows, g * HEAD_DIM:(g + 1) * HEAD_DIM] = y[:, :HEAD_DIM].astype(BF16)
                y_ref[seq + r0:seq + r0 + rows, g * HEAD_DIM:(g + 1) * HEAD_DIM] = y[:, HEAD_DIM:].astype(BF16)

    even = jnp.dot(cos_ref[0], y_ref[0:seq, :], preferred_element_type=F32)
    odd = jnp.dot(sin_ref[0], y_ref[seq:2 * seq, :], preferred_element_type=F32)
    lo = pl.multiple_of(m * DFT_TILE, DFT_TILE)
    o_ref[0, pl.ds(lo, DFT_TILE), :] = (even - odd)[:DFT_TILE].astype(BF16)
    mirrored = jnp.dot(flip_ref[...], (even + odd).astype(BF16), preferred_element_type=F32)
    hi = pl.multiple_of(seq - lo - DFT_TILE, DFT_TILE)
    o_ref[0, pl.ds(hi, DFT_TILE), :] = mirrored.astype(BF16)


def _fourier(uf, tables, batch, seq):
    chan, cos_t, sin_t, flip = tables
    u3 = uf.reshape(batch, seq, FOURIER_WIDTH)
    rows = DFT_TILE + DFT_EXTRA
    out = pl.pallas_call(
        functools.partial(_fourier_kernel, seq=seq),
        grid=(batch, seq // 2 // DFT_TILE),
        in_specs=[pl.BlockSpec((1, seq, FOURIER_WIDTH), lambda b, m: (b, 0, 0)),
                  pl.BlockSpec((HEAD_DIM, 2 * HEAD_DIM), lambda b, m: (0, 0)),
                  pl.BlockSpec((1, rows, seq), lambda b, m: (m, 0, 0)),
                  pl.BlockSpec((1, rows, seq), lambda b, m: (m, 0, 0)),
                  pl.BlockSpec((DFT_TILE, rows), lambda b, m: (0, 0))],
        out_specs=pl.BlockSpec((1, seq, FOURIER_WIDTH), lambda b, m: (b, 0, 0)),
        out_shape=jax.ShapeDtypeStruct((batch, seq, FOURIER_WIDTH), BF16),
        scratch_shapes=[pltpu.VMEM((2 * seq, FOURIER_WIDTH), BF16)],
        compiler_params=_cparams("parallel", "arbitrary"),
        name="fourier",
    )(u3, chan, cos_t, sin_t, flip)
    return out.reshape(batch * seq, FOURIER_WIDTH)


def _pool_kernel(u_ref, w_ref, scale_ref, o_ref, pad_ref, *, seq):
    rows = 512
    zeros = jnp.zeros((POOL_PAD, HEAD_DIM), F32)
    pad_ref[0:POOL_PAD, :] = zeros
    pad_ref[POOL_PAD + seq:2 * POOL_PAD + seq, :] = zeros
    for g, window in enumerate(POOL_WINDOWS):
        cols = slice(g * HEAD_DIM, (g + 1) * HEAD_DIM)
        half = window // 2
        pad_ref[POOL_PAD:POOL_PAD + seq, :] = u_ref[0, :, cols].astype(F32)
        for r0 in range(0, seq, rows):
            base = POOL_PAD + r0
            acc = pad_ref[base - half:base - half + rows, :]
            for off in range(-half + 1, half):
                acc = acc + pad_ref[base + off:base + off + rows, :]
            pos = r0 + lax.broadcasted_iota(jnp.int32, (rows, 1), 0)
            cnt = jnp.minimum(pos + half - 1, seq - 1) - jnp.maximum(pos - half, 0) + 1
            d = acc / cnt.astype(F32) - pad_ref[base:base + rows, :]
            y = jnp.dot(d.astype(BF16), w_ref[g], preferred_element_type=F32) * scale_ref[:, cols]
            o_ref[0, r0:r0 + rows, cols] = y.astype(BF16)


def _pool(up, w, layer, scale, batch, seq):
    u3 = up.reshape(batch, seq, POOL_WIDTH)
    out = pl.pallas_call(
        functools.partial(_pool_kernel, seq=seq),
        grid=(batch,),
        in_specs=[pl.BlockSpec((1, seq, POOL_WIDTH), lambda b: (b, 0, 0)),
                  _layer_block((len(POOL_WINDOWS), HEAD_DIM, HEAD_DIM), layer),
                  pl.BlockSpec((1, POOL_WIDTH), lambda b: (0, 0))],
        out_specs=pl.BlockSpec((1, seq, POOL_WIDTH), lambda b: (b, 0, 0)),
        out_shape=jax.ShapeDtypeStruct((batch, seq, POOL_WIDTH), BF16),
        scratch_shapes=[pltpu.VMEM((seq + 2 * POOL_PAD, HEAD_DIM), F32)],
        compiler_params=_cparams("parallel"),
        name="pool",
    )(u3, w, scale)
    return out.reshape(batch * seq, POOL_WIDTH)


def _merge_kernel(x_ref, o1_ref, o2_ref, o3_ref, l1_ref, l2_ref, l3_ref, four_ref, pool_ref, gate_ref,
                  wba_ref, wbf_ref, wbp_ref, wout_ref, xo_ref, *scratch):
    n_dilated = len(ATTN_GROUPS) - 1
    oslabs = scratch[:n_dilated * HEADS_PER_GROUP]
    lslabs = scratch[n_dilated * HEADS_PER_GROUP:-2]
    attn_ref, acc_ref = scratch[-2:]
    tm = x_ref.shape[0]
    o_refs = (o1_ref, o2_ref, o3_ref)
    l_refs = (l1_ref, l2_ref, l3_ref)
    f = jnp.dot(four_ref[...], wbf_ref[...], preferred_element_type=F32)
    p = jnp.dot(pool_ref[...], wbp_ref[...], preferred_element_type=F32)
    gate = lambda b: 0.5 * jnp.tanh(0.5 * gate_ref[:, b * D_MODEL:(b + 1) * D_MODEL].astype(F32)) + 0.5
    acc_ref[...] = gate(1) * f + gate(2) * p
    dilated = [(g, d) for g, (_, d) in enumerate(ATTN_GROUPS) if d > 1]
    for g, d in dilated:
        for r in range(d):
            lslabs[g - 1][pl.ds(r, tm // d, stride=d), :] = l_refs[g][:, r * LANES:(r + 1) * LANES]
    lses = [l1_ref[...]] + [ref[...] for ref in lslabs]
    top = functools.reduce(jnp.maximum, lses)
    es = [jnp.exp(l - top) for l in lses]
    inv = 1.0 / functools.reduce(lambda a, b: a + b, es)
    weights = [e * inv for e in es]
    for j in range(HEADS_PER_GROUP):
        cols = slice(j * HEAD_DIM, (j + 1) * HEAD_DIM)
        for g, d in dilated:
            slab = oslabs[(g - 1) * HEADS_PER_GROUP + j]
            for r in range(d):
                lo = r * GROUP_WIDTH + j * HEAD_DIM
                slab[pl.ds(r, tm // d, stride=d), :] = o_refs[g][:, lo:lo + HEAD_DIM].astype(F32)
        outs = [o1_ref[:, cols].astype(F32)] + [oslabs[(g - 1) * HEADS_PER_GROUP + j][...] for g, _ in dilated]
        attn_ref[:, cols] = sum(w[:, j:j + 1] * o for w, o in zip(weights, outs)).astype(BF16)
    a = jnp.dot(attn_ref[...], wba_ref[...], preferred_element_type=F32)
    merged = gate(0) * a + acc_ref[...]
    xo_ref[...] = x_ref[...] + jnp.dot(merged.astype(BF16), wout_ref[...], preferred_element_type=F32)


def _merge(x2, os_, ls_, four, pool, gates, wba, wbf, wbp, wout, layer, tm=512):
    n = x2.shape[0]
    row = lambda i: (i, 0)
    return pl.pallas_call(
        _merge_kernel,
        grid=(n // tm,),
        in_specs=[pl.BlockSpec((tm, D_MODEL), row)]
        + [pl.BlockSpec((tm // d, d * GROUP_WIDTH), row) for _, d in ATTN_GROUPS]
        + [pl.BlockSpec((tm // d, d * LANES), row) for _, d in ATTN_GROUPS]
        + [pl.BlockSpec((tm, FOURIER_WIDTH), row), pl.BlockSpec((tm, POOL_WIDTH), row),
           pl.BlockSpec((tm, GATE_WIDTH), row)]
        + [_layer_block((GROUP_WIDTH, D_MODEL), layer)] * 3
        + [_layer_block((D_MODEL, D_MODEL), layer)],
        out_specs=pl.BlockSpec((tm, D_MODEL), row),
        out_shape=jax.ShapeDtypeStruct((n, D_MODEL), F32),
        scratch_shapes=[pltpu.VMEM((tm, LANES), F32)] * ((len(ATTN_GROUPS) - 1) * (HEADS_PER_GROUP + 1))
        + [pltpu.VMEM((tm, GROUP_WIDTH), BF16), pltpu.VMEM((tm, D_MODEL), F32)],
        compiler_params=_cparams("parallel"),
        name="merge",
    )(x2, *os_, *ls_, four, pool, gates, wba, wbf, wbp, wout)


FFN_CHUNK = 256


def _silu(a):
    return a * (0.5 * jnp.tanh(0.5 * a) + 0.5)


def _ffn_kernel(x_ref, gain_ref, w1_ref, w3_ref, w2_ref, o_ref, h_ref, g_ref):
    h_ref[...] = _rms(x_ref[...], gain_ref[...]).astype(BF16)
    ffn = w1_ref.shape[1]
    for c0 in range(0, ffn, FFN_CHUNK):
        a = jnp.dot(h_ref[...], w1_ref[:, c0:c0 + FFN_CHUNK], preferred_element_type=F32)
        b = jnp.dot(h_ref[...], w3_ref[:, c0:c0 + FFN_CHUNK], preferred_element_type=F32)
        g_ref[:, c0:c0 + FFN_CHUNK] = (_silu(a) * b).astype(BF16)
    o_ref[...] = x_ref[...] + jnp.dot(g_ref[...], w2_ref[...], preferred_element_type=F32)


def _dense_ffn(x2, gain, w1, w3, w2, layer, tm=1024):
    n = x2.shape[0]
    ffn = w1.shape[2]
    row = lambda i: (i, 0)
    return pl.pallas_call(
        _ffn_kernel,
        grid=(n // tm,),
        in_specs=[pl.BlockSpec((tm, D_MODEL), row), pl.BlockSpec((1, D_MODEL), lambda i: (0, 0)),
                  _layer_block((D_MODEL, ffn), layer, resident=True),
                  _layer_block((D_MODEL, ffn), layer, resident=True),
                  _layer_block((ffn, D_MODEL), layer, resident=True)],
        out_specs=pl.BlockSpec((tm, D_MODEL), row),
        out_shape=jax.ShapeDtypeStruct((n, D_MODEL), F32),
        scratch_shapes=[pltpu.VMEM((tm, D_MODEL), BF16), pltpu.VMEM((tm, ffn), BF16)],
        compiler_params=_cparams("parallel"),
        name="dense_ffn",
    )(x2, gain, w1, w3, w2)


R_E0, R_E1, R_G0, R_G1, R_RANK0, R_RANK1 = 0, 1, 2, 3, 4, 5


def _router_kernel(x_ref, g_ref, rw_ref, rb_ref, meta_ref, metat_ref, cnt_ref, *, tm):
    h = _rms(x_ref[...], g_ref[...])
    logits = jnp.dot(h.astype(BF16), rw_ref[...], preferred_element_type=F32) + rb_ref[...]
    lane = lax.broadcasted_iota(jnp.int32, (tm, LANES), 1)
    logits = jnp.where(lane < N_EXPERTS, logits, -jnp.inf)
    v0 = jnp.max(logits, axis=-1, keepdims=True)
    e0 = jnp.min(jnp.where(logits == v0, lane, LANES), axis=-1, keepdims=True)
    rest = jnp.where(lane == e0, -jnp.inf, logits)
    v1 = jnp.max(rest, axis=-1, keepdims=True)
    e1 = jnp.min(jnp.where(rest == v1, lane, LANES), axis=-1, keepdims=True)
    t = jnp.exp(v1 - v0)
    gate0 = 1.0 / (1.0 + t)
    gate1 = t / (1.0 + t)

    picked = jnp.logical_or(lane == e0, lane == e1)
    tri = (lax.broadcasted_iota(jnp.int32, (tm, tm), 1) < lax.broadcasted_iota(jnp.int32, (tm, tm), 0))
    before = jnp.dot(tri.astype(BF16), picked.astype(BF16), preferred_element_type=F32)
    rank0 = jnp.sum(jnp.where(lane == e0, before, 0.0), axis=-1, keepdims=True)
    rank1 = jnp.sum(jnp.where(lane == e1, before, 0.0), axis=-1, keepdims=True)

    meta = jnp.zeros((tm, LANES), F32)
    for ln, val in ((R_E0, e0.astype(F32)), (R_E1, e1.astype(F32)), (R_G0, gate0), (R_G1, gate1),
                    (R_RANK0, rank0), (R_RANK1, rank1)):
        meta = jnp.where(lane == ln, val, meta)
    meta_ref[...] = meta
    metat_ref[0] = jnp.transpose(meta)[:8]
    cnt_ref[0] = jnp.sum(picked.astype(F32), axis=0, keepdims=True)


def _router(x2, g, rw, rb, tm):
    n = x2.shape[0]
    rw_p = jnp.zeros((D_MODEL, LANES), BF16).at[:, :N_EXPERTS].set(rw.astype(BF16))
    rb_p = jnp.zeros((1, LANES), F32).at[0, :N_EXPERTS].set(rb.astype(F32))
    return pl.pallas_call(
        functools.partial(_router_kernel, tm=tm),
        grid=(n // tm,),
        in_specs=[pl.BlockSpec((tm, D_MODEL), lambda i: (i, 0)),
                  pl.BlockSpec((1, D_MODEL), lambda i: (0, 0)),
                  pl.BlockSpec((D_MODEL, LANES), lambda i: (0, 0)),
                  pl.BlockSpec((1, LANES), lambda i: (0, 0))],
        out_specs=[pl.BlockSpec((tm, LANES), lambda i: (i, 0)),
                   pl.BlockSpec((1, 8, tm), lambda i: (i, 0, 0)),
                   pl.BlockSpec((1, 1, LANES), lambda i: (i, 0, 0))],
        out_shape=[jax.ShapeDtypeStruct((n, LANES), F32),
                   jax.ShapeDtypeStruct((n // tm, 8, tm), F32),
                   jax.ShapeDtypeStruct((n // tm, 1, LANES), F32)],
        compiler_params=_cparams("parallel"),
        name="router",
    )(x2, g, rw_p, rb_p)


MOE_TILE = 512
RUN_ALIGN = 16
RUN_CHUNKS = (512, 256, 128, 64, 32, 16)
SORT_ROWS = -(-(MOE_TILE * TOP_K + N_EXPERTS * (RUN_ALIGN - 1)) // LANES) * LANES


def _run_copies(i, loc_ref, glob_ref, len_ref, make_copy, action):
    for e in range(N_EXPERTS):
        n = len_ref[i * N_EXPERTS + e]
        lo = loc_ref[i * N_EXPERTS + e]
        go = glob_ref[i * N_EXPERTS + e]
        for size in RUN_CHUNKS:
            done = n & ~(2 * size - 1)

            @pl.when((n & size) != 0)
            def _():
                action(make_copy(pl.multiple_of(lo + done, RUN_ALIGN), pl.multiple_of(go + done, RUN_ALIGN), size))


def _run_base(i, loc_ref, expert):
    base = jnp.zeros_like(expert)
    for e in range(N_EXPERTS):
        base = jnp.where(expert == e, loc_ref[i * N_EXPERTS + e], base)
    return base


def _dispatch_kernel(loc_ref, glob_ref, len_ref, x_ref, g_ref, metat_ref, xs_hbm, sorted_ref, zero_ref, sem):
    i = pl.program_id(0)
    n_tiles = pl.num_programs(0)

    @pl.when(i == 0)
    def _():
        zero_ref[...] = jnp.zeros_like(zero_ref)

        def make_copy(lo, go, size):
            return pltpu.make_async_copy(zero_ref.at[pl.ds(lo, size), :], xs_hbm.at[pl.ds(go, size), :], sem.at[2])

        _run_copies(n_tiles, loc_ref, glob_ref, len_ref, make_copy, lambda c: c.start())
        _run_copies(n_tiles, loc_ref, glob_ref, len_ref, make_copy, lambda c: c.wait())
        last = n_tiles * N_EXPERTS + N_EXPERTS - 1
        first_unused = (glob_ref[last] + len_ref[last]) // EXPERT_BLOCK

        def clear(j, carry):
            block = make_copy(0, pl.multiple_of(j * EXPERT_BLOCK, EXPERT_BLOCK), EXPERT_BLOCK)
            block.start()
            block.wait()
            return carry

        lax.fori_loop(first_unused, xs_hbm.shape[0] // EXPERT_BLOCK, clear, 0)

    h = _rms(x_ref[...], g_ref[...]).astype(BF16)
    rec = metat_ref[0]
    slot0 = _run_base(i, loc_ref, rec[R_E0:R_E0 + 1].astype(jnp.int32)) + rec[R_RANK0:R_RANK0 + 1].astype(jnp.int32)
    slot1 = _run_base(i, loc_ref, rec[R_E1:R_E1 + 1].astype(jnp.int32)) + rec[R_RANK1:R_RANK1 + 1].astype(jnp.int32)
    rows = lax.broadcasted_iota(jnp.int32, (SORT_ROWS, MOE_TILE), 0)
    perm = jnp.logical_or(rows == slot0, rows == slot1).astype(BF16)
    sorted_ref[i % 2] = jnp.dot(perm, h, preferred_element_type=F32).astype(BF16)

    def copies(tile, action):
        def make_copy(lo, go, size):
            return pltpu.make_async_copy(sorted_ref.at[tile % 2, pl.ds(lo, size), :],
                                         xs_hbm.at[pl.ds(go, size), :], sem.at[tile % 2])
        _run_copies(tile, loc_ref, glob_ref, len_ref, make_copy, action)

    copies(i, lambda c: c.start())

    @pl.when(i > 0)
    def _():
        copies(i - 1, lambda c: c.wait())

    @pl.when(i == n_tiles - 1)
    def _():
        copies(i, lambda c: c.wait())


def _dispatch(x2, g, metat, loc, glob, plen, total):
    n = x2.shape[0]
    return pl.pallas_call(
        _dispatch_kernel,
        grid_spec=pltpu.PrefetchScalarGridSpec(
            num_scalar_prefetch=3,
            grid=(n // MOE_TILE,),
            in_specs=[pl.BlockSpec((MOE_TILE, D_MODEL), lambda i, *_: (i, 0)),
                      pl.BlockSpec((1, D_MODEL), lambda i, *_: (0, 0)),
                      pl.BlockSpec((1, 8, MOE_TILE), lambda i, *_: (i, 0, 0))],
            out_specs=pl.BlockSpec(memory_space=pl.ANY),
            scratch_shapes=[pltpu.VMEM((2, SORT_ROWS, D_MODEL), BF16), pltpu.VMEM((EXPERT_BLOCK, D_MODEL), BF16),
                            pltpu.SemaphoreType.DMA((3,))]),
        out_shape=jax.ShapeDtypeStruct((total, D_MODEL), BF16),
        compiler_params=_cparams("arbitrary"),
        name="moe_dispatch",
    )(loc, glob, plen, x2, g, metat)


MOE_CHUNK = 512


def _expert_kernel(blk_e_ref, n_used_ref, xs_ref, w1_ref, w3_ref, w2_ref, ys_ref, act_ref):
    j = pl.program_id(0)

    @pl.when(j < n_used_ref[0])
    def _():
        h = xs_ref[...]
        ffn = w1_ref.shape[2]
        for c0 in range(0, ffn, MOE_CHUNK):
            a = jnp.dot(h, w1_ref[0, :, c0:c0 + MOE_CHUNK], preferred_element_type=F32)
            b = jnp.dot(h, w3_ref[0, :, c0:c0 + MOE_CHUNK], preferred_element_type=F32)
            act_ref[:, c0:c0 + MOE_CHUNK] = (_silu(a) * b).astype(BF16)
        ys_ref[...] = jnp.dot(act_ref[...], w2_ref[0], preferred_element_type=F32).astype(BF16)

    @pl.when(j >= n_used_ref[0])
    def _():
        ys_ref[...] = jnp.zeros_like(ys_ref)


def _experts(xs, w1, w3, w2, layer, blk_e, n_used):
    total = xs.shape[0]
    ffn = w1.shape[3]
    n_blocks = total // EXPERT_BLOCK
    expert = lambda j, be, nu: (layer, be[j], 0, 0)
    return pl.pallas_call(
        _expert_kernel,
        grid_spec=pltpu.PrefetchScalarGridSpec(
            num_scalar_prefetch=2,
            grid=(n_blocks,),
            in_specs=[pl.BlockSpec((EXPERT_BLOCK, D_MODEL), lambda j, be, nu: (jnp.minimum(j, nu[0] - 1), 0)),
                      pl.BlockSpec((None, 1, D_MODEL, ffn), expert),
                      pl.BlockSpec((None, 1, D_MODEL, ffn), expert),
                      _resident((None, 1, ffn, D_MODEL), expert)],
            out_specs=pl.BlockSpec((EXPERT_BLOCK, D_MODEL), lambda j, be, nu: (j, 0)),
            scratch_shapes=[pltpu.VMEM((EXPERT_BLOCK, ffn), BF16)]),
        out_shape=jax.ShapeDtypeStruct((total, D_MODEL), BF16),
        compiler_params=_cparams("arbitrary"),
        name="moe_experts",
    )(blk_e, n_used, xs, w1, w3, w2)


def _combine_kernel(loc_ref, glob_ref, len_ref, x_ref, meta_ref, ys_hbm, o_ref, buf_ref, sem):
    i = pl.program_id(0)

    def copies(tile, action):
        def make_copy(lo, go, size):
            return pltpu.make_async_copy(ys_hbm.at[pl.ds(go, size), :],
                                         buf_ref.at[tile % 2, pl.ds(lo, size), :], sem.at[tile % 2])
        _run_copies(tile, loc_ref, glob_ref, len_ref, make_copy, action)

    @pl.when(i == 0)
    def _():
        copies(i, lambda c: c.start())

    @pl.when(i + 1 < pl.num_programs(0))
    def _():
        copies(i + 1, lambda c: c.start())

    rec = meta_ref[...]
    slot0 = _run_base(i, loc_ref, rec[:, R_E0:R_E0 + 1].astype(jnp.int32)) + rec[:, R_RANK0:R_RANK0 + 1].astype(jnp.int32)
    slot1 = _run_base(i, loc_ref, rec[:, R_E1:R_E1 + 1].astype(jnp.int32)) + rec[:, R_RANK1:R_RANK1 + 1].astype(jnp.int32)
    cols = lax.broadcasted_iota(jnp.int32, (MOE_TILE, SORT_ROWS), 1)
    weight = jnp.where(cols == slot0, rec[:, R_G0:R_G0 + 1],
                       jnp.where(cols == slot1, rec[:, R_G1:R_G1 + 1], 0.0)).astype(BF16)
    copies(i, lambda c: c.wait())
    filled = loc_ref[i * N_EXPERTS + N_EXPERTS - 1] + len_ref[i * N_EXPERTS + N_EXPERTS - 1]
    rows = lax.broadcasted_iota(jnp.int32, (SORT_ROWS, 1), 0)
    picked = jnp.where(rows < filled, buf_ref[i % 2], jnp.zeros((), BF16))
    o_ref[...] = x_ref[...] + jnp.dot(weight, picked, preferred_element_type=F32)


def _combine(x2, meta, ys, loc, glob, plen):
    n = x2.shape[0]
    return pl.pallas_call(
        _combine_kernel,
        grid_spec=pltpu.PrefetchScalarGridSpec(
            num_scalar_prefetch=3,
            grid=(n // MOE_TILE,),
            in_specs=[pl.BlockSpec((MOE_TILE, D_MODEL), lambda i, *_: (i, 0)),
                      pl.BlockSpec((MOE_TILE, LANES), lambda i, *_: (i, 0)),
                      pl.BlockSpec(memory_space=pl.ANY)],
            out_specs=pl.BlockSpec((MOE_TILE, D_MODEL), lambda i, *_: (i, 0)),
            scratch_shapes=[pltpu.VMEM((2, SORT_ROWS, D_MODEL), BF16), pltpu.SemaphoreType.DMA((2,))]),
        out_shape=jax.ShapeDtypeStruct((n, D_MODEL), F32),
        compiler_params=_cparams("arbitrary"),
        name="moe_combine",
    )(loc, glob, plen, x2, meta, ys)


def _moe(x2, g, rw, rb, w1, w3, w2, layer):
    n = x2.shape[0]
    n_tiles = n // MOE_TILE
    meta, metat, cnt = _router(x2, g, rw, rb, MOE_TILE)
    cnt = cnt[:, 0, :N_EXPERTS].astype(jnp.int32)
    plen = (cnt + RUN_ALIGN - 1) // RUN_ALIGN * RUN_ALIGN
    loc = jnp.cumsum(plen, axis=1) - plen
    filled = jnp.sum(plen, axis=0)
    padded = (filled + EXPERT_BLOCK - 1) // EXPERT_BLOCK * EXPERT_BLOCK
    pend = jnp.cumsum(padded)
    glob = (pend - padded)[None, :] + jnp.cumsum(plen, axis=0) - plen
    loc = jnp.concatenate([loc, jnp.zeros((1, N_EXPERTS), jnp.int32)])
    glob = jnp.concatenate([glob, (pend - padded + filled)[None, :]])
    plen = jnp.concatenate([plen, (padded - filled)[None, :]])
    n_blocks = -(-(n * TOP_K + n_tiles * N_EXPERTS * (RUN_ALIGN - 1)) // EXPERT_BLOCK) + N_EXPERTS
    blk_start = jnp.arange(n_blocks, dtype=jnp.int32) * EXPERT_BLOCK
    blk_e = jnp.minimum(jnp.sum(blk_start[:, None] >= pend[None, :], axis=1), N_EXPERTS - 1).astype(jnp.int32)
    n_used = (pend[-1] // EXPERT_BLOCK).astype(jnp.int32).reshape(1)
    loc, glob, plen = (a.reshape(-1).astype(jnp.int32) for a in (loc, glob, plen))
    xs = _dispatch(x2, g, metat, loc, glob, plen, n_blocks * EXPERT_BLOCK)
    ys = _experts(xs, w1, w3, w2, layer, blk_e, n_used)
    return _combine(x2, meta, ys, loc, glob, plen)


def kernel(x, norm_mix_g, w_in, q_norm_g, k_norm_g, rel_bias, pool_w, pool_scale, w_branch_attn,
           w_branch_fourier, w_branch_pool, w_out, norm_ffn_g, ffn_w1, ffn_w3, ffn_w2,
           router_w, router_b, moe_w1, moe_w3, moe_w2):
    batch, seq, _ = x.shape
    depth = w_in.shape[0]
    n = batch * seq
    x2 = x.reshape(n, D_MODEL)
    biases = [_attn_bias(rel_bias, g) for g in range(len(ATTN_GROUPS))]
    dft = _dft_tables(seq)
    row = lambda v: v.reshape(1, -1).astype(F32)
    (w_in, pool_w, w_branch_attn, w_branch_fourier, w_branch_pool, w_out, ffn_w1, ffn_w3, ffn_w2,
     moe_w1, moe_w3, moe_w2) = (w.astype(BF16) for w in (
         w_in, pool_w, w_branch_attn, w_branch_fourier, w_branch_pool, w_out, ffn_w1, ffn_w3, ffn_w2,
         moe_w1, moe_w3, moe_w2))
    for layer in range(depth):
        *qkvs, uf, up, gates = _in_proj(x2, row(norm_mix_g[layer]), w_in, layer,
                                        row(q_norm_g[layer]) * (HEAD_DIM ** -0.5), row(k_norm_g[layer]))
        os_, ls_ = [], []
        for g in range(len(ATTN_GROUPS)):
            o, lse = _attention_group(qkvs[g], biases[g], g, batch, seq)
            os_.append(o)
            ls_.append(lse)
        four = _fourier(uf, dft, batch, seq)
        pool = _pool(up, pool_w, layer, row(pool_scale[layer]), batch, seq)
        j = layer // 2
        x2 = _merge(x2, os_, ls_, four, pool, gates, w_branch_attn, w_branch_fourier, w_branch_pool, w_out, layer)
        if layer % 2 == 0:
            x2 = _dense_ffn(x2, row(norm_ffn_g[layer]), ffn_w1, ffn_w3, ffn_w2, j)
        else:
            x2 = _moe(x2, row(norm_ffn_g[layer]), router_w[j], router_b[j], moe_w1, moe_w3, moe_w2, j)
    return x2.reshape(batch, seq, D_MODEL)
```
